```python
import jax, jax.numpy as jnp
from jax import lax
import numpy as np

D_MODEL = 1024
BATCH = 1
SEQ = 16384
DEPTH = 4
DEC_BATCH = 32
DEC_SEQ = 16
PAST_LEN = 2048

CHUNK = 64
N_A = DEPTH // 2
N_B = DEPTH - N_A
CONV_W = 3
N_HEADS = 16
HEAD_DIM = 64
D_ATTN = N_HEADS * HEAD_DIM
D_FF = 4 * D_MODEL
Q_BLOCK = 128
RMS_EPS = 1e-6

kernel_name = "yoco_shortconv_fox_stream_step"


def rmsnorm(x, g):
    xf = x.astype(jnp.float32)
    y = xf * lax.rsqrt(jnp.mean(xf * xf, axis=-1, keepdims=True) + RMS_EPS)
    return (y * g.astype(jnp.float32)).astype(x.dtype)


def short_conv(h, state, w_in, w_conv, w_out):
    bcx = h @ w_in
    gb, gc, xv = jnp.split(bcx, 3, axis=-1)
    u = gc * xv
    L = u.shape[1]
    u_pad = jnp.concatenate([state.astype(u.dtype), u], axis=1)
    conv = sum(w_conv[i] * u_pad[:, i:i + L] for i in range(CONV_W))
    y = (gb * conv) @ w_out
    return y, u_pad[:, -(CONV_W - 1):]


def sq_relu_mlp(h, w_up, w_down):
    a = jax.nn.relu(h @ w_up)
    return (a * a) @ w_down


def fox_attend(q, k, v, c_q, c_k, q_pos, k_pos):
    s = jnp.einsum('bqhd,bkhd->bhqk', q, k).astype(jnp.float32) * (HEAD_DIM ** -0.5)
    bias = jnp.transpose(c_q, (0, 2, 1))[:, :, :, None] - jnp.transpose(c_k, (0, 2, 1))[:, :, None, :]
    mask = k_pos[None, :] <= q_pos[:, None]
    s = jnp.where(mask[None, None], s + bias, -jnp.inf)
    p = jax.nn.softmax(s, axis=-1)
    return jnp.einsum('bhqk,bkhd->bqhd', p.astype(v.dtype), v)


def setup_inputs(seed: int = 0) -> dict:
    key = jax.random.key(seed)
    ks = jax.random.split(key, 24)
    nrm = lambda k, shape, scale: jax.random.normal(k, shape, jnp.float32) * scale
    return {
        "x_prompt": nrm(ks[0], (BATCH, SEQ, D_MODEL), 1.0),
        "x_sample": nrm(ks[1], (DEC_BATCH, DEC_SEQ, D_MODEL), 1.0),
        "cache_k": nrm(ks[2], (DEC_BATCH, PAST_LEN, N_HEADS, HEAD_DIM), 1.0),
        "cache_v": nrm(ks[3], (DEC_BATCH, PAST_LEN, N_HEADS, HEAD_DIM), 1.0),
        "cache_logf": jax.nn.log_sigmoid(3.0 + nrm(ks[4], (DEC_BATCH, PAST_LEN, N_HEADS), 0.5)),
        "state_conv": nrm(ks[5], (N_A, DEC_BATCH, CONV_W - 1, D_MODEL), 1.0),
        "norm_mix": 1.0 + nrm(ks[6], (DEPTH, D_MODEL), 0.02),
        "norm_mlp": 1.0 + nrm(ks[7], (DEPTH, D_MODEL), 0.02),
        "conv_in": nrm(ks[8], (N_A, D_MODEL, 3 * D_MODEL), D_MODEL ** -0.5),
        "conv_w": nrm(ks[9], (N_A, CONV_W, D_MODEL), CONV_W ** -0.5),
        "conv_out": nrm(ks[10], (N_A, D_MODEL, D_MODEL), D_MODEL ** -0.5),
        "norm_kv": 1.0 + nrm(ks[11], (D_MODEL,), 0.02),
        "w_k": nrm(ks[12], (D_MODEL, D_ATTN), D_MODEL ** -0.5),
        "w_v": nrm(ks[13], (D_MODEL, D_ATTN), D_MODEL ** -0.5),
        "w_f": nrm(ks[14], (D_MODEL, N_HEADS), 0.1 * D_MODEL ** -0.5),
        "b_f": 3.0 + nrm(ks[15], (N_HEADS,), 0.5),
        "w_q": nrm(ks[16], (N_B, D_MODEL, D_ATTN), D_MODEL ** -0.5),
        "w_o": nrm(ks[17], (N_B, D_ATTN, D_MODEL), D_ATTN ** -0.5),
        "mlp_up": nrm(ks[18], (DEPTH, D_MODEL, D_FF), D_MODEL ** -0.5),
        "mlp_down": nrm(ks[19], (DEPTH, D_FF, D_MODEL), D_FF ** -0.5),
        "norm_final": 1.0 + nrm(ks[20], (D_MODEL,), 0.02),
    }


def reference(x_prompt, x_sample, cache_k, cache_v, cache_logf, state_conv,
              norm_mix, norm_mlp, conv_in, conv_w, conv_out,
              norm_kv, w_k, w_v, w_f, b_f, w_q, w_o,
              mlp_up, mlp_down, norm_final):

    def trunk(x, conv_state, past):
        bsz, L, _ = x.shape
        new_conv = []
        shared = None
        for l in range(DEPTH):
            h = rmsnorm(x, norm_mix[l])
            if l < N_A:
                y, st = short_conv(h, conv_state[l], conv_in[l], conv_w[l], conv_out[l])
                new_conv.append(st)
            else:
                j = l - N_A
                k_all, v_all, c_all, k_pos, q_off = shared
                q = (h @ w_q[j]).reshape(bsz, L, N_HEADS, HEAD_DIM)
                if past is None:
                    nb = L // Q_BLOCK
                    qb = jnp.transpose(q.reshape(bsz, nb, Q_BLOCK, N_HEADS, HEAD_DIM), (1, 0, 2, 3, 4))
                    cqb = jnp.transpose(c_all.reshape(bsz, nb, Q_BLOCK, N_HEADS), (1, 0, 2, 3))
                    posb = jnp.arange(L, dtype=jnp.int32).reshape(nb, Q_BLOCK)
                    ob = lax.map(lambda a: fox_attend(a[0], k_all, v_all, a[1], c_all, a[2], k_pos),
                                 (qb, cqb, posb))
                    attn = jnp.transpose(ob, (1, 0, 2, 3, 4)).reshape(bsz, L, D_ATTN)
                else:
                    q_pos = q_off + jnp.arange(L, dtype=jnp.int32)
                    attn = fox_attend(q, k_all, v_all, c_all[:, q_off:], c_all, q_pos, k_pos)
                    attn = attn.reshape(bsz, L, D_ATTN)
                y = attn @ w_o[j]
            x = x + y
            x = x + sq_relu_mlp(rmsnorm(x, norm_mlp[l]), mlp_up[l], mlp_down[l])
            if l == N_A - 1:
                hk = rmsnorm(x, norm_kv)
                k_new = (hk @ w_k).reshape(bsz, L, N_HEADS, HEAD_DIM)
                v_new = (hk @ w_v).reshape(bsz, L, N_HEADS, HEAD_DIM)
                logf_new = jax.nn.log_sigmoid((hk @ w_f + b_f).astype(jnp.float32))
                if past is None:
                    k_all, v_all, logf_all, q_off = k_new, v_new, logf_new, 0
                else:
                    k_p, v_p, lf_p = past
                    q_off = k_p.shape[1]
                    k_all = jnp.concatenate([k_p, k_new.astype(k_p.dtype)], axis=1)
                    v_all = jnp.concatenate([v_p, v_new.astype(v_p.dtype)], axis=1)
                    logf_all = jnp.concatenate([lf_p.astype(jnp.float32), logf_new], axis=1)
                c_all = jnp.cumsum(logf_all, axis=1)
                k_pos = jnp.arange(k_all.shape[1], dtype=jnp.int32)
                shared = (k_all, v_all, c_all, k_pos, q_off)
        return rmsnorm(x, norm_final), k_new, v_new, logf_new, jnp.stack(new_conv)

    conv0 = jnp.zeros((N_A, x_prompt.shape[0], CONV_W - 1, D_MODEL), x_prompt.dtype)
    y_prompt, k_prompt, v_prompt, logf_prompt, conv_prompt = trunk(x_prompt, conv0, None)
    y_sample, k_sample, v_sample, logf_sample, conv_sample = trunk(
        x_sample, state_conv, (cache_k, cache_v, cache_logf))
    return (y_prompt, y_sample, k_prompt, v_prompt, logf_prompt, conv_prompt,
            k_sample, v_sample, logf_sample, conv_sample)
```

```python
import functools
import math

import numpy as np
import jax
import jax.numpy as jnp
from jax import lax
from jax.experimental import pallas as pl
from jax.experimental.pallas import tpu as pltpu

D_MODEL = 1024
N_HEADS = 16
HEAD_DIM = 64
D_FF = 4 * D_MODEL
N_A = 2
DEPTH = 4
RMS_EPS = 1e-6
LOG2E = math.log2(math.e)
Q_SCALE = HEAD_DIM ** -0.5 * LOG2E
NEG_BIG = -1e30

LANES = 128
ROW_TILE = 512
ATT_TILE = 512
VMEM_LIMIT = 56 * 1024 * 1024

CQ_LANE0 = 64
CK_LANE0 = 112
ONE_LANE_V = 64

bf16 = jnp.bfloat16
f32 = jnp.float32


def _dot(a, b):
    return jnp.dot(a, b, preferred_element_type=f32)


def _dot_nt(a, b):
    return lax.dot_general(a, b, (((1,), (1,)), ((), ())), preferred_element_type=f32)


def _rms(x, g):
    ms = jnp.mean(x * x, axis=-1, keepdims=True)
    return x * lax.rsqrt(ms + RMS_EPS) * g


def _split3(x):
    hi = x.astype(bf16)
    r1 = x - hi.astype(f32)
    mid = r1.astype(bf16)
    lo = (r1 - mid.astype(f32)).astype(bf16)
    return hi, mid, lo


def _tri(n):
    r = lax.broadcasted_iota(jnp.int32, (n, n), 0)
    c = lax.broadcasted_iota(jnp.int32, (n, n), 1)
    return jnp.where(c <= r, 1.0, 0.0).astype(bf16)


def _prefix_sum(t, x):
    hi, mid, lo = _split3(x)
    return _dot(t, hi) + _dot(t, mid) + _dot(t, lo)


def _log_sigmoid(z):
    return jnp.minimum(z, 0.0) - jnp.log1p(jnp.exp(-jnp.abs(z)))


def _resident(shape):
    nd = len(shape)
    return pl.BlockSpec(shape, lambda *_: (0,) * nd, pipeline_mode=pl.Buffered(1))


def _params(n_axes):
    return pltpu.CompilerParams(dimension_semantics=("arbitrary",) * n_axes,
                                vmem_limit_bytes=VMEM_LIMIT)


def _placement_constants():
    pk = np.zeros((3, LANES, N_HEADS * LANES), np.float32)
    ck_const = np.zeros((1, N_HEADS * LANES), np.float32)
    pq = np.zeros((3, LANES, LANES), np.float32)
    cq_const = np.zeros((1, LANES), np.float32)
    for j in range(3):
        cq_const[0, CK_LANE0 + j] = 1.0
        for h in range(N_HEADS):
            pk[j, h, h * LANES + CK_LANE0 + j] = -1.0
            ck_const[0, h * LANES + CQ_LANE0 + 16 * j + h] = 1.0
            pq[j, h, CQ_LANE0 + 16 * j + h] = 1.0
    expand = np.zeros((N_HEADS * 16, N_HEADS), np.float32)
    for h in range(N_HEADS):
        expand[h * 16:(h + 1) * 16, h] = 1.0
    return (jnp.asarray(pk, bf16), jnp.asarray(ck_const), jnp.asarray(pq, bf16),
            jnp.asarray(cq_const), jnp.asarray(expand, bf16))


def _pack_heads(x, extras_of_head, out_ref):
    lane = lax.broadcasted_iota(jnp.int32, (x.shape[0], LANES), 1)
    for h in range(N_HEADS):
        pair = x[:, (h // 2) * LANES:(h // 2 + 1) * LANES]
        base = pair if h % 2 == 0 else pltpu.roll(pair, HEAD_DIM, axis=1)
        out_ref[h] = jnp.where(lane < HEAD_DIM, base, extras_of_head(h)).astype(bf16)


def _mixer_kernel(*refs, tm, carry_rows, seq_len):
    if carry_rows:
        x_ref, g_ref, win_ref, cw_ref, wout_ref, o_ref, u_ref, u_scr = refs
    else:
        x_ref, g_ref, win_ref, cw_ref, wout_ref, s1_ref, s2_ref, o_ref, u_ref, u_scr = refs

    @pl.when(pl.program_id(0) == 0)
    def _():
        u_scr[0:8, :] = jnp.zeros((8, D_MODEL), f32)

    x = x_ref[...]
    h = _rms(x, g_ref[...]).astype(bf16)
    bcx = _dot(h, win_ref[...])
    gb = bcx[:, 0:D_MODEL]
    u = bcx[:, D_MODEL:2 * D_MODEL] * bcx[:, 2 * D_MODEL:3 * D_MODEL]
    u_scr[8:8 + tm, :] = u
    u1 = u_scr[7:7 + tm, :]
    u2 = u_scr[6:6 + tm, :]
    if carry_rows:
        tail = u_scr[tm:tm + 8, :]
        u_scr[0:8, :] = tail
        u_ref[...] = tail
    else:
        pos = lax.broadcasted_iota(jnp.int32, (tm, 1), 0) % seq_len
        u1 = jnp.where(pos < 1, s1_ref[...], u1)
        u2 = jnp.where(pos < 2, s2_ref[...], u2)
        u_ref[...] = u
    cw = cw_ref[...]
    conv = cw[0:1, :] * u2 + cw[1:2, :] * u1 + cw[2:3, :] * u
    o_ref[...] = x + _dot((gb * conv).astype(bf16), wout_ref[...])


def _mixer(x, g, w_in, cw, w_out, state_rows=None, seq_len=None):
    n = x.shape[0]
    tm = min(ROW_TILE, n)
    carry = state_rows is None
    row = pl.BlockSpec((tm, D_MODEL), lambda i: (i, 0))
    in_specs = [row, _resident((1, D_MODEL)), _resident((D_MODEL, 3 * D_MODEL)),
                _resident((3, D_MODEL)), _resident((D_MODEL, D_MODEL))]
    args = [x, g, w_in, cw, w_out]
    if carry:
        u_spec = pl.BlockSpec((8, D_MODEL), lambda i: (0, 0))
        u_shape = jax.ShapeDtypeStruct((8, D_MODEL), f32)
    else:
        assert n == tm
        in_specs += [row, row]
        args += list(state_rows)
        u_spec = row
        u_shape = jax.ShapeDtypeStruct((n, D_MODEL), f32)
    return pl.pallas_call(
        functools.partial(_mixer_kernel, tm=tm, carry_rows=carry, seq_len=seq_len),
        grid=(n // tm,),
        in_specs=in_specs,
        out_specs=[row, u_spec],
        out_shape=[jax.ShapeDtypeStruct((n, D_MODEL), f32), u_shape],
        scratch_shapes=[pltpu.VMEM((tm + 8, D_MODEL), f32)],
        compiler_params=_params(1),
        name="mixer",
    )(*args)


def _mlp_kernel(*refs, has_attn, final_norm):
    refs = list(refs)
    x_ref = refs.pop(0)
    if has_attn:
        attn_ref = refs.pop(0)
        wo_ref = refs.pop(0)
    g_ref, wup_ref, wdn_ref = refs[0:3]
    gf_ref = refs[3] if final_norm else None
    o_ref = refs[-1]

    x = x_ref[...]
    if has_attn:
        x = x + _dot(attn_ref[...], wo_ref[...])
    h = _rms(x, g_ref[...]).astype(bf16)
    a = jnp.maximum(_dot(h, wup_ref[...]), 0.0)
    y = x + _dot((a * a).astype(bf16), wdn_ref[...])
    if final_norm:
        y = _rms(y, gf_ref[...])
    o_ref[...] = y


def _mlp(x, g, w_up, w_down, attn=None, w_o=None, g_final=None):
    n = x.shape[0]
    tm = min(ROW_TILE, n)
    row = pl.BlockSpec((tm, D_MODEL), lambda i: (i, 0))
    in_specs, args = [row], [x]
    if attn is not None:
        in_specs += [row, _resident((D_MODEL, D_MODEL))]
        args += [attn, w_o]
    in_specs += [_resident((1, D_MODEL)), _resident((D_MODEL, D_FF)), _resident((D_FF, D_MODEL))]
    args += [g, w_up, w_down]
    if g_final is not None:
        in_specs.append(_resident((1, D_MODEL)))
        args.append(g_final)
    return pl.pallas_call(
        functools.partial(_mlp_kernel, has_attn=attn is not None, final_norm=g_final is not None),
        grid=(n // tm,),
        in_specs=in_specs,
        out_specs=row,
        out_shape=jax.ShapeDtypeStruct((n, D_MODEL), f32),
        compiler_params=_params(1),
        name="mlp",
    )(*args)


def _kv_kernel(*refs, tm, pack):
    if pack:
        (x_ref, g_ref, wk_ref, wv_ref, wf_ref, bf_ref, pk_ref, ckc_ref,
         k_ref, v_ref, lf_ref, c_ref, kp_ref, vp_ref, carry) = refs
    else:
        x_ref, g_ref, wk_ref, wv_ref, wf_ref, bf_ref, k_ref, v_ref, lf_ref = refs

    hk = _rms(x_ref[...], g_ref[...]).astype(bf16)
    k = _dot(hk, wk_ref[...])
    v = _dot(hk, wv_ref[...])
    z = _dot(hk, wf_ref[...]) + bf_ref[...]
    lane = lax.broadcasted_iota(jnp.int32, (tm, LANES), 1)
    logf = jnp.where(lane < N_HEADS, _log_sigmoid(z), 0.0)
    k_ref[...] = k
    v_ref[...] = v
    lf_ref[...] = logf[:, 0:N_HEADS]
    if not pack:
        return

    @pl.when(pl.program_id(0) == 0)
    def _():
        carry[...] = jnp.zeros_like(carry)

    c = _prefix_sum(_tri(tm), logf) + carry[0:1, :]
    carry[0:1, :] = c[tm - 1:tm, :]
    c2 = c * LOG2E
    c_ref[...] = c2
    hi, mid, lo = _split3(c2)
    extras = _dot(hi, pk_ref[0]) + _dot(mid, pk_ref[1]) + _dot(lo, pk_ref[2]) + ckc_ref[...]
    _pack_heads(k, lambda h: extras[:, h * LANES:(h + 1) * LANES], kp_ref)
    ones = jnp.where(lane == ONE_LANE_V, 1.0, 0.0)
    _pack_heads(v, lambda h: ones, vp_ref)


def _kv(x, g, w_k, w_v, w_f, b_f, pk=None, ck_const=None):
    n = x.shape[0]
    tm = min(ROW_TILE, n)
    pack = pk is not None
    row = pl.BlockSpec((tm, D_MODEL), lambda i: (i, 0))
    heads = pl.BlockSpec((N_HEADS, tm, LANES), lambda i: (0, i, 0))
    in_specs = [row, _resident((1, D_MODEL)), _resident((D_MODEL, D_MODEL)), _resident((D_MODEL, D_MODEL)),
                _resident((D_MODEL, LANES)), _resident((1, LANES))]
    args = [x, g, w_k, w_v, w_f, b_f]
    out_specs = [row, row, pl.BlockSpec((tm, N_HEADS), lambda i: (i, 0))]
    out_shape = [jax.ShapeDtypeStruct((n, D_MODEL), f32), jax.ShapeDtypeStruct((n, D_MODEL), f32),
                 jax.ShapeDtypeStruct((n, N_HEADS), f32)]
    scratch = []
    if pack:
        in_specs += [_resident((3, LANES, N_HEADS * LANES)), _resident((1, N_HEADS * LANES))]
        args += [pk, ck_const]
        out_specs += [pl.BlockSpec((tm, LANES), lambda i: (i, 0)), heads, heads]
        out_shape += [jax.ShapeDtypeStruct((n, LANES), f32),
                      jax.ShapeDtypeStruct((N_HEADS, n, LANES), bf16),
                      jax.ShapeDtypeStruct((N_HEADS, n, LANES), bf16)]
        scratch = [pltpu.VMEM((8, LANES), f32)]
    return pl.pallas_call(
        functools.partial(_kv_kernel, tm=tm, pack=pack),
        grid=(n // tm,),
        in_specs=in_specs,
        out_specs=out_specs,
        out_shape=out_shape,
        scratch_shapes=scratch,
        compiler_params=_params(1),
        name="kv",
    )(*args)


def _qproj_kernel(*refs, pack):
    if pack:
        x_ref, g_ref, wq_ref, c_ref, pq_ref, cqc_ref, o_ref = refs
    else:
        x_ref, g_ref, wq_ref, o_ref = refs
    h = _rms(x_ref[...], g_ref[...]).astype(bf16)
    q = _dot(h, wq_ref[...]) * Q_SCALE
    if not pack:
        o_ref[...] = q
        return
    hi, mid, lo = _split3(c_ref[...])
    extras = _dot(hi, pq_ref[0]) + _dot(mid, pq_ref[1]) + _dot(lo, pq_ref[2]) + cqc_ref[...]
    _pack_heads(q, lambda h: extras, o_ref)


def _qproj(x, g, w_q, c2=None, pq=None, cq_const=None):
    n = x.shape[0]
    tm = min(ROW_TILE, n)
    pack = c2 is not None
    row = pl.BlockSpec((tm, D_MODEL), lambda i: (i, 0))
    in_specs = [row, _resident((1, D_MODEL)), _resident((D_MODEL, D_MODEL))]
    args = [x, g, w_q]
    if pack:
        in_specs += [pl.BlockSpec((tm, LANES), lambda i: (i, 0)), _resident((3, LANES, LANES)),
                     _resident((1, LANES))]
        args += [c2, pq, cq_const]
        out_spec = pl.BlockSpec((N_HEADS, tm, LANES), lambda i: (0, i, 0))
        out_shape = jax.ShapeDtypeStruct((N_HEADS, n, LANES), bf16)
    else:
        out_spec = row
        out_shape = jax.ShapeDtypeStruct((n, D_MODEL), f32)
    return pl.pallas_call(
        functools.partial(_qproj_kernel, pack=pack),
        grid=(n // tm,),
        in_specs=in_specs,
        out_specs=out_spec,
        out_shape=out_shape,
        compiler_params=_params(1),
        name="qproj",
    )(*args)


def _flash_kernel(q_ref, k_ref, v_ref, o_ref, m_scr, acc_scr, *, t):
    i = pl.program_id(1)
    for hh in range(2):
        m_scr[hh] = jnp.full((t, LANES), NEG_BIG, f32)
        acc_scr[hh] = jnp.zeros((t, LANES), f32)

    def kv_step(j, diagonal):
        start = pl.multiple_of(j * t, t)
        for hh in range(2):
            s = _dot_nt(q_ref[hh], k_ref[hh, pl.ds(start, t), :])
            if diagonal:
                r = lax.broadcasted_iota(jnp.int32, (t, t), 0)
                c = lax.broadcasted_iota(jnp.int32, (t, t), 1)
                s = jnp.where(c <= r, s, NEG_BIG)
            m_prev = m_scr[hh]
            m_new = jnp.maximum(m_prev, jnp.max(s, axis=1, keepdims=True))
            p = jnp.exp2(s - jnp.tile(m_new, (1, t // LANES)))
            pv = _dot(p.astype(bf16), v_ref[hh, pl.ds(start, t), :])
            acc_scr[hh] = jnp.exp2(m_prev - m_new) * acc_scr[hh] + pv
            m_scr[hh] = m_new

    def body(j, carry):
        kv_step(j, False)
        return carry

    lax.fori_loop(0, i, body, 0)
    kv_step(i, True)

    outs = []
    for hh in range(2):
        acc = acc_scr[hh]
        outs.append(acc / acc[:, ONE_LANE_V:ONE_LANE_V + 1])
    lane = lax.broadcasted_iota(jnp.int32, (t, LANES), 1)
    o_ref[...] = jnp.where(lane < HEAD_DIM, outs[0], pltpu.roll(outs[1], HEAD_DIM, axis=1)).astype(bf16)


def _flash(qp, kp, vp):
    n = qp.shape[1]
    t = ATT_TILE
    resident = pl.BlockSpec((2, n, LANES), lambda p, i: (p, 0, 0))
    return pl.pallas_call(
        functools.partial(_flash_kernel, t=t),
        grid=(N_HEADS // 2, n // t),
        in_specs=[pl.BlockSpec((2, t, LANES), lambda p, i: (p, i, 0)), resident, resident],
        out_specs=pl.BlockSpec((t, LANES), lambda p, i: (i, p)),
        out_shape=jax.ShapeDtypeStruct((n, D_MODEL), bf16),
        scratch_shapes=[pltpu.VMEM((2, t, LANES), f32), pltpu.VMEM((2, t, LANES), f32)],
        compiler_params=_params(2),
        name="flash",
    )(qp, kp, vp)


def _cum_kernel(past_ref, new_ref, o_ref, *, past_len, new_len, chunk):
    t = _tri(chunk)
    carry = jnp.zeros((1, N_HEADS), f32)
    for s in range(0, past_len, chunk):
        c = _prefix_sum(t, past_ref[0, s:s + chunk, :]) + carry
        o_ref[0, s:s + chunk, :] = c * LOG2E
        carry = c[chunk - 1:chunk, :]
    c = _prefix_sum(_tri(new_len), new_ref[0]) + carry
    o_ref[0, past_len:past_len + new_len, :] = c * LOG2E


def _cum(past_logf, new_logf):
    b, past_len, _ = past_logf.shape
    new_len = new_logf.shape[1]
    total = past_len + new_len
    return pl.pallas_call(
        functools.partial(_cum_kernel, past_len=past_len, new_len=new_len, chunk=512),
        grid=(b,),
        in_specs=[pl.BlockSpec((1, past_len, N_HEADS), lambda i: (i, 0, 0)),
                  pl.BlockSpec((1, new_len, N_HEADS), lambda i: (i, 0, 0))],
        out_specs=pl.BlockSpec((1, total, N_HEADS), lambda i: (i, 0, 0)),
        out_shape=jax.ShapeDtypeStruct((b, total, N_HEADS), f32),
        compiler_params=_params(1),
        name="cum",
    )(past_logf, new_logf)


def _expand_heads(e, x):
    hi, mid, lo = _split3(x)
    return _dot_nt(e, hi) + _dot_nt(e, mid) + _dot_nt(e, lo)


def _decode_attn_kernel(q_ref, kc_ref, vc_ref, kn_ref, vn_ref, c_ref, e_ref, o_ref, *, past_len, new_len):
    rows = N_HEADS * new_len
    q = q_ref[0]
    row_head = lax.broadcasted_iota(jnp.int32, (rows, D_MODEL), 0) // new_len
    col_head = lax.broadcasted_iota(jnp.int32, (rows, D_MODEL), 1) // HEAD_DIM
    qbd = jnp.where(row_head == col_head, jnp.tile(q, (N_HEADS, 1)), 0.0).astype(bf16)
    e = e_ref[...]
    c2 = c_ref[0]
    ck_new = _expand_heads(e, c2[past_len:past_len + new_len, :])
    t_of_row = lax.broadcasted_iota(jnp.int32, (rows, new_len), 0) % new_len
    s_idx = lax.broadcasted_iota(jnp.int32, (rows, new_len), 1)
    cq = jnp.sum(jnp.where(s_idx == t_of_row, ck_new, 0.0), axis=1, keepdims=True)

    lg = _dot_nt(qbd, kc_ref[0].astype(bf16)) + cq - _expand_heads(e, c2[0:past_len, :])
    lg_new = jnp.where(s_idx <= t_of_row, _dot_nt(qbd, kn_ref[0].astype(bf16)) + cq - ck_new, NEG_BIG)
    m = jnp.maximum(jnp.max(lg, axis=1, keepdims=True), jnp.max(lg_new, axis=1, keepdims=True))
    p = jnp.exp2(lg - m)
    p_new = jnp.exp2(lg_new - m)
    denom = jnp.sum(p, axis=1, keepdims=True) + jnp.sum(p_new, axis=1, keepdims=True)
    o = (_dot(p.astype(bf16), vc_ref[0].astype(bf16))
         + _dot(p_new.astype(bf16), vn_ref[0].astype(bf16))) / denom
    lane = lax.broadcasted_iota(jnp.int32, (new_len, LANES), 1)
    for pp in range(N_HEADS // 2):
        even = o[(2 * pp) * new_len:(2 * pp + 1) * new_len, pp * LANES:(pp + 1) * LANES]
        odd = o[(2 * pp + 1) * new_len:(2 * pp + 2) * new_len, pp * LANES:(pp + 1) * LANES]
        o_ref[0, :, pp * LANES:(pp + 1) * LANES] = jnp.where(lane < HEAD_DIM, even, odd).astype(bf16)


def _decode_attn(q, cache_k, cache_v, k_new, v_new, c2, expand):
    b, past_len, _ = cache_k.shape
    new_len = q.shape[1]
    cache = pl.BlockSpec((1, past_len, D_MODEL), lambda i: (i, 0, 0))
    new = pl.BlockSpec((1, new_len, D_MODEL), lambda i: (i, 0, 0))
    return pl.pallas_call(
        functools.partial(_decode_attn_kernel, past_len=past_len, new_len=new_len),
        grid=(b,),
        in_specs=[new, cache, cache, new, new,
                  pl.BlockSpec((1, past_len + new_len, N_HEADS), lambda i: (i, 0, 0)),
                  _resident((N_HEADS * new_len, N_HEADS))],
        out_specs=new,
        out_shape=jax.ShapeDtypeStruct((b, new_len, D_MODEL), bf16),
        compiler_params=_params(1),
        name="decode_attn",
    )(q, cache_k, cache_v, k_new, v_new, c2, expand)


def kernel(x_prompt, x_sample, cache_k, cache_v, cache_logf, state_conv, norm_mix, norm_mlp, conv_in, conv_w,
           conv_out, norm_kv, w_k, w_v, w_f, b_f, w_q, w_o, mlp_up, mlp_down, norm_final):
    assert x_prompt.shape[0] == 1 and x_prompt.shape[2] == D_MODEL
    seq = x_prompt.shape[1]
    dec_b, dec_s, _ = x_sample.shape
    past_len = cache_k.shape[1]
    assert seq % ROW_TILE == 0 and seq % ATT_TILE == 0 and (dec_b * dec_s) % 8 == 0

    pk, ck_const, pq, cq_const, expand = _placement_constants()
    row = lambda a: a.reshape(1, -1)
    conv_in_b, conv_out_b = conv_in.astype(bf16), conv_out.astype(bf16)
    w_k_b, w_v_b, w_q_b, w_o_b = w_k.astype(bf16), w_v.astype(bf16), w_q.astype(bf16), w_o.astype(bf16)
    up_b, down_b = mlp_up.astype(bf16), mlp_down.astype(bf16)
    w_f_b = jnp.pad(w_f, ((0, 0), (0, LANES - N_HEADS))).astype(bf16)
    b_f_p = jnp.pad(b_f, (0, LANES - N_HEADS)).reshape(1, LANES)

    def mlp(x, l, **kw):
        return _mlp(x, row(norm_mlp[l]), up_b[l], down_b[l], **kw)

    x = x_prompt.reshape(seq, D_MODEL)
    tails = []
    for l in range(N_A):
        x, tail = _mixer(x, row(norm_mix[l]), conv_in_b[l], conv_w[l], conv_out_b[l])
        tails.append(tail[6:8])
        x = mlp(x, l)
    k_p, v_p, logf_p, c2_p, kp, vp = _kv(x, row(norm_kv), w_k_b, w_v_b, w_f_b, b_f_p, pk, ck_const)
    for l in range(N_A, DEPTH):
        j = l - N_A
        qp = _qproj(x, row(norm_mix[l]), w_q_b[j], c2_p, pq, cq_const)
        attn = _flash(qp, kp, vp)
        x = mlp(x, l, attn=attn, w_o=w_o_b[j], g_final=row(norm_final) if l == DEPTH - 1 else None)
    y_prompt = x.reshape(1, seq, D_MODEL)
    conv_prompt = jnp.stack(tails).reshape(N_A, 1, 2, D_MODEL)

    n_s = dec_b * dec_s
    x = x_sample.reshape(n_s, D_MODEL)
    new_states = []
    for l in range(N_A):
        st = state_conv[l]
        pad = jnp.zeros((dec_b, dec_s - 2, D_MODEL), f32)
        s2 = jnp.concatenate([st, pad], axis=1).reshape(n_s, D_MODEL)
        s1 = jnp.concatenate([st[:, 1:2], pad, pad[:, 0:1]], axis=1).reshape(n_s, D_MODEL)
        x, u = _mixer(x, row(norm_mix[l]), conv_in_b[l], conv_w[l], conv_out_b[l],
                      state_rows=(s1, s2), seq_len=dec_s)
        new_states.append(u.reshape(dec_b, dec_s, D_MODEL)[:, dec_s - 2:])
        x = mlp(x, l)
    k_s, v_s, logf_s = _kv(x, row(norm_kv), w_k_b, w_v_b, w_f_b, b_f_p)
    c2_s = _cum(cache_logf, logf_s.reshape(dec_b, dec_s, N_HEADS))
    k_s3, v_s3 = k_s.reshape(dec_b, dec_s, D_MODEL), v_s.reshape(dec_b, dec_s, D_MODEL)
    cache_k3 = cache_k.reshape(dec_b, past_len, D_MODEL)
    cache_v3 = cache_v.reshape(dec_b, past_len, D_MODEL)
    for l in range(N_A, DEPTH):
        j = l - N_A
        q = _qproj(x, row(norm_mix[l]), w_q_b[j])
        attn = _decode_attn(q.reshape(dec_b, dec_s, D_MODEL), cache_k3, cache_v3, k_s3, v_s3, c2_s, expand)
        x = mlp(x, l, attn=attn.reshape(n_s, D_MODEL), w_o=w_o_b[j],
                g_final=row(norm_final) if l == DEPTH - 1 else None)
    y_sample = x.reshape(dec_b, dec_s, D_MODEL)

    return (y_prompt, y_sample,
            k_p.reshape(1, seq, N_HEADS, HEAD_DIM), v_p.reshape(1, seq, N_HEADS, HEAD_DIM),
            logf_p.reshape(1, seq, N_HEADS), conv_prompt,
            k_s.reshape(dec_b, dec_s, N_HEADS, HEAD_DIM), v_s.reshape(dec_b, dec_s, N_HEADS, HEAD_DIM),
            logf_s.reshape(dec_b, dec_s, N_HEADS), jnp.stack(new_states))
```

```python
import functools
import math

import numpy as np
import jax
import jax.numpy as jnp
from jax import lax
from jax.experimental import pallas as pl
from jax.experimental.pallas import tpu as pltpu

D_MODEL = 1024
N_HEADS = 16
HEAD_DIM = 64
D_FF = 4 * D_MODEL
N_A = 2
DEPTH = 4
RMS_EPS = 1e-6
LOG2E = math.log2(math.e)
Q_SCALE = HEAD_DIM ** -0.5 * LOG2E
NEG_BIG = -1e30

LANES = 128
ROW_TILE = 512
ATT_TILE = 512
VMEM_LIMIT = 56 * 1024 * 1024

CQ_LANE0 = 64
CK_LANE0 = 112
ONE_LANE_V = 64

bf16 = jnp.bfloat16
f32 = jnp.float32


def _dot(a, b):
    return jnp.dot(a, b, preferred_element_type=f32)


def _dot_nt(a, b):
    return lax.dot_general(a, b, (((1,), (1,)), ((), ())), preferred_element_type=f32)


def _rms(x, g):
    ms = jnp.mean(x * x, axis=-1, keepdims=True)
    return x * lax.rsqrt(ms + RMS_EPS) * g


def _split3(x):
    hi = x.astype(bf16)
    r1 = x - hi.astype(f32)
    mid = r1.astype(bf16)
    lo = (r1 - mid.astype(f32)).astype(bf16)
    return hi, mid, lo


def _tri(n):
    r = lax.broadcasted_iota(jnp.int32, (n, n), 0)
    c = lax.broadcasted_iota(jnp.int32, (n, n), 1)
    return jnp.where(c <= r, 1.0, 0.0).astype(bf16)


def _prefix_sum(t, x):
    hi, mid, lo = _split3(x)
    return _dot(t, hi) + _dot(t, mid) + _dot(t, lo)


def _log_sigmoid(z):
    return jnp.minimum(z, 0.0) - jnp.log1p(jnp.exp(-jnp.abs(z)))


def _resident(shape):
    nd = len(shape)
    return pl.BlockSpec(shape, lambda *_: (0,) * nd, pipeline_mode=pl.Buffered(1))


def _params(n_axes):
    return pltpu.CompilerParams(dimension_semantics=("arbitrary",) * n_axes,
                                vmem_limit_bytes=VMEM_LIMIT)


def _placement_constants():
    pk = np.zeros((3, LANES, N_HEADS * LANES), np.float32)
    ck_const = np.zeros((1, N_HEADS * LANES), np.float32)
    pq = np.zeros((3, LANES, LANES), np.float32)
    cq_const = np.zeros((1, LANES), np.float32)
    for j in range(3):
        cq_const[0, CK_LANE0 + j] = 1.0
        for h in range(N_HEADS):
            pk[j, h, h * LANES + CK_LANE0 + j] = -1.0
            ck_const[0, h * LANES + CQ_LANE0 + 16 * j + h] = 1.0
            pq[j, h, CQ_LANE0 + 16 * j + h] = 1.0
    expand = np.zeros((N_HEADS * 16, N_HEADS), np.float32)
    for h in range(N_HEADS):
        expand[h * 16:(h + 1) * 16, h] = 1.0
    return (jnp.asarray(pk, bf16), jnp.asarray(ck_const), jnp.asarray(pq, bf16),
            jnp.asarray(cq_const), jnp.asarray(expand, bf16))


def _pack_heads(x, extras_of_head, out_ref):
    lane = lax.broadcasted_iota(jnp.int32, (x.shape[0], LANES), 1)
    for h in range(N_HEADS):
        pair = x[:, (h // 2) * LANES:(h // 2 + 1) * LANES]
        base = pair if h % 2 == 0 else pltpu.roll(pair, HEAD_DIM, axis=1)
        out_ref[h] = jnp.where(lane < HEAD_DIM, base, extras_of_head(h)).astype(bf16)


def _mixer_kernel(*refs, tm, carry_rows, seq_len):
    if carry_rows:
        x_ref, g_ref, win_ref, cw_ref, wout_ref, o_ref, u_ref, u_scr = refs
    else:
        x_ref, g_ref, win_ref, cw_ref, wout_ref, s1_ref, s2_ref, o_ref, u_ref, u_scr = refs

    @pl.when(pl.program_id(0) == 0)
    def _():
        u_scr[0:8, :] = jnp.zeros((8, D_MODEL), f32)

    x = x_ref[...]
    h = _rms(x, g_ref[...]).astype(bf16)
    bcx = _dot(h, win_ref[...])
    gb = bcx[:, 0:D_MODEL]
    u = bcx[:, D_MODEL:2 * D_MODEL] * bcx[:, 2 * D_MODEL:3 * D_MODEL]
    u_scr[8:8 + tm, :] = u
    u1 = u_scr[7:7 + tm, :]
    u2 = u_scr[6:6 + tm, :]
    if carry_rows:
        tail = u_scr[tm:tm + 8, :]
        u_scr[0:8, :] = tail
        u_ref[...] = tail
    else:
        pos = lax.broadcasted_iota(jnp.int32, (tm, 1), 0) % seq_len
        u1 = jnp.where(pos < 1, s1_ref[...], u1)
        u2 = jnp.where(pos < 2, s2_ref[...], u2)
        u_ref[...] = u
    cw = cw_ref[...]
    conv = cw[0:1, :] * u2 + cw[1:2, :] * u1 + cw[2:3, :] * u
    o_ref[...] = x + _dot((gb * conv).astype(bf16), wout_ref[...])


def _mixer(x, g, w_in, cw, w_out, state_rows=None, seq_len=None):
    n = x.shape[0]
    tm = min(ROW_TILE, n)
    carry = state_rows is None
    row = pl.BlockSpec((tm, D_MODEL), lambda i: (i, 0))
    in_specs = [row, _resident((1, D_MODEL)), _resident((D_MODEL, 3 * D_MODEL)),
                _resident((3, D_MODEL)), _resident((D_MODEL, D_MODEL))]
    args = [x, g, w_in, cw, w_out]
    if carry:
        u_spec = pl.BlockSpec((8, D_MODEL), lambda i: (0, 0))
        u_shape = jax.ShapeDtypeStruct((8, D_MODEL), f32)
    else:
        assert n == tm
        in_specs += [row, row]
        args += list(state_rows)
        u_spec = row
        u_shape = jax.ShapeDtypeStruct((n, D_MODEL), f32)
    return pl.pallas_call(
        functools.partial(_mixer_kernel, tm=tm, carry_rows=carry, seq_len=seq_len),
        grid=(n // tm,),
        in_specs=in_specs,
        out_specs=[row, u_spec],
        out_shape=[jax.ShapeDtypeStruct((n, D_MODEL), f32), u_shape],
        scratch_shapes=[pltpu.VMEM((tm + 8, D_MODEL), f32)],
        compiler_params=_params(1),
        name="mixer",
    )(*args)


def _mlp_kernel(*refs, has_attn, final_norm):
    refs = list(refs)
    x_ref = refs.pop(0)
    if has_attn:
        attn_ref = refs.pop(0)
        wo_ref = refs.pop(0)
    g_ref, wup_ref, wdn_ref = refs[0:3]
    gf_ref = refs[3] if final_norm else None
    o_ref = refs[-1]

    x = x_ref[...]
    if has_attn:
        x = x + _dot(attn_ref[...], wo_ref[...])
    h = _rms(x, g_ref[...]).astype(bf16)
    a = jnp.maximum(_dot(h, wup_ref[...]), 0.0)
    y = x + _dot((a * a).astype(bf16), wdn_ref[...])
    if final_norm:
        y = _rms(y, gf_ref[...])
    o_ref[...] = y


def _mlp(x, g, w_up, w_down, attn=None, w_o=None, g_final=None):
    n = x.shape[0]
    tm = min(ROW_TILE, n)
    row = pl.BlockSpec((tm, D_MODEL), lambda i: (i, 0))
    in_specs, args = [row], [x]
    if attn is not None:
        in_specs += [row, _resident((D_MODEL, D_MODEL))]
        args += [attn, w_o]
    in_specs += [_resident((1, D_MODEL)), _resident((D_MODEL, D_FF)), _resident((D_FF, D_MODEL))]
    args += [g, w_up, w_down]
    if g_final is not None:
        in_specs.append(_resident((1, D_MODEL)))
        args.append(g_final)
    return pl.pallas_call(
        functools.partial(_mlp_kernel, has_attn=attn is not None, final_norm=g_final is not None),
        grid=(n // tm,),
        in_specs=in_specs,
        out_specs=row,
        out_shape=jax.ShapeDtypeStruct((n, D_MODEL), f32),
        compiler_params=_params(1),
        name="mlp",
    )(*args)


def _kv_kernel(*refs, tm, pack):
    if pack:
        (x_ref, g_ref, wk_ref, wv_ref, wf_ref, bf_ref, pk_ref, ckc_ref,
         k_ref, v_ref, lf_ref, c_ref, kp_ref, vp_ref, carry) = refs
    else:
        x_ref, g_ref, wk_ref, wv_ref, wf_ref, bf_ref, k_ref, v_ref, lf_ref = refs

    hk = _rms(x_ref[...], g_ref[...]).astype(bf16)
    k = _dot(hk, wk_ref[...])
    v = _dot(hk, wv_ref[...])
    z = _dot(hk, wf_ref[...]) + bf_ref[...]
    lane = lax.broadcasted_iota(jnp.int32, (tm, LANES), 1)
    logf = jnp.where(lane < N_HEADS, _log_sigmoid(z), 0.0)
    k_ref[...] = k
    v_ref[...] = v
    lf_ref[...] = logf[:, 0:N_HEADS]
    if not pack:
        return

    @pl.when(pl.program_id(0) == 0)
    def _():
        carry[...] = jnp.zeros_like(carry)

    c = _prefix_sum(_tri(tm), logf) + carry[0:1, :]
    carry[0:1, :] = c[tm - 1:tm, :]
    c2 = c * LOG2E
    c_ref[...] = c2
    hi, mid, lo = _split3(c2)
    extras = _dot(hi, pk_ref[0]) + _dot(mid, pk_ref[1]) + _dot(lo, pk_ref[2]) + ckc_ref[...]
    _pack_heads(k, lambda h: extras[:, h * LANES:(h + 1) * LANES], kp_ref)
    ones = jnp.where(lane == ONE_LANE_V, 1.0, 0.0)
    _pack_heads(v, lambda h: ones, vp_ref)


def _kv(x, g, w_k, w_v, w_f, b_f, pk=None, ck_const=None):
    n = x.shape[0]
    tm = min(ROW_TILE, n)
    pack = pk is not None
    row = pl.BlockSpec((tm, D_MODEL), lambda i: (i, 0))
    heads = pl.BlockSpec((N_HEADS, tm, LANES), lambda i: (0, i, 0))
    in_specs = [row, _resident((1, D_MODEL)), _resident((D_MODEL, D_MODEL)), _resident((D_MODEL, D_MODEL)),
                _resident((D_MODEL, LANES)), _resident((1, LANES))]
    args = [x, g, w_k, w_v, w_f, b_f]
    out_specs = [row, row, pl.BlockSpec((tm, N_HEADS), lambda i: (i, 0))]
    out_shape = [jax.ShapeDtypeStruct((n, D_MODEL), f32), jax.ShapeDtypeStruct((n, D_MODEL), f32),
                 jax.ShapeDtypeStruct((n, N_HEADS), f32)]
    scratch = []
    if pack:
        in_specs += [_resident((3, LANES, N_HEADS * LANES)), _resident((1, N_HEADS * LANES))]
        args += [pk, ck_const]
        out_specs += [pl.BlockSpec((tm, LANES), lambda i: (i, 0)), heads, heads]
        out_shape += [jax.ShapeDtypeStruct((n, LANES), f32),
                      jax.ShapeDtypeStruct((N_HEADS, n, LANES), bf16),
                      jax.ShapeDtypeStruct((N_HEADS, n, LANES), bf16)]
        scratch = [pltpu.VMEM((8, LANES), f32)]
    return pl.pallas_call(
        functools.partial(_kv_kernel, tm=tm, pack=pack),
        grid=(n // tm,),
        in_specs=in_specs,
        out_specs=out_specs,
        out_shape=out_shape,
        scratch_shapes=scratch,
        compiler_params=_params(1),
        name="kv",
    )(*args)


def _qproj_kernel(*refs, pack):
    if pack:
        x_ref, g_ref, wq_ref, c_ref, pq_ref, cqc_ref, o_ref = refs
    else:
        x_ref, g_ref, wq_ref, o_ref = refs
    h = _rms(x_ref[...], g_ref[...]).astype(bf16)
    q = _dot(h, wq_ref[...]) * Q_SCALE
    if not pack:
        o_ref[...] = q
        return
    hi, mid, lo = _split3(c_ref[...])
    extras = _dot(hi, pq_ref[0]) + _dot(mid, pq_ref[1]) + _dot(lo, pq_ref[2]) + cqc_ref[...]
    _pack_heads(q, lambda h: extras, o_ref)


def _qproj(x, g, w_q, c2=None, pq=None, cq_const=None):
    n = x.shape[0]
    tm = min(ROW_TILE, n)
    pack = c2 is not None
    row = pl.BlockSpec((tm, D_MODEL), lambda i: (i, 0))
    in_specs = [row, _resident((1, D_MODEL)), _resident((D_MODEL, D_MODEL))]
    args = [x, g, w_q]
    if pack:
        in_specs += [pl.BlockSpec((tm, LANES), lambda i: (i, 0)), _resident((3, LANES, LANES)),
                     _resident((1, LANES))]
        args += [c2, pq, cq_const]
        out_spec = pl.BlockSpec((N_HEADS, tm, LANES), lambda i: (0, i, 0))
        out_shape = jax.ShapeDtypeStruct((N_HEADS, n, LANES), bf16)
    else:
        out_spec = row
        out_shape = jax.ShapeDtypeStruct((n, D_MODEL), f32)
    return pl.pallas_call(
        functools.partial(_qproj_kernel, pack=pack),
        grid=(n // tm,),
        in_specs=in_specs,
        out_specs=out_spec,
        out_shape=out_shape,
        compiler_params=_params(1),
        name="qproj",
    )(*args)


def _flash_kernel(q_ref, k_ref, v_ref, o_ref, m_scr, acc_scr, s_even, s_odd, *, t):
    i = pl.program_id(1)
    for hh in range(2):
        m_scr[hh] = jnp.full((t, LANES), NEG_BIG, f32)
        acc_scr[hh] = jnp.zeros((t, LANES), f32)

    def logits(j, hh):
        start = pl.multiple_of(j * t, t)
        return _dot_nt(q_ref[hh], k_ref[hh, pl.ds(start, t), :])

    def step(j, s_cur, s_next, diagonal):
        start = pl.multiple_of(j * t, t)
        for hh in range(2):
            s = s_cur[hh]
            if diagonal:
                r = lax.broadcasted_iota(jnp.int32, (t, t), 0)
                c = lax.broadcasted_iota(jnp.int32, (t, t), 1)
                s = jnp.where(c <= r, s, NEG_BIG)
            else:
                s_next[hh] = logits(j + 1, hh)
            m_prev = m_scr[hh]
            m_new = jnp.maximum(m_prev, jnp.max(s, axis=1, keepdims=True))
            p = jnp.exp2(s - jnp.tile(m_new, (1, t // LANES)))
            pv = _dot(p.astype(bf16), v_ref[hh, pl.ds(start, t), :])
            acc_scr[hh] = jnp.exp2(m_prev - m_new) * acc_scr[hh] + pv
            m_scr[hh] = m_new

    for hh in range(2):
        s_even[hh] = logits(0, hh)

    def pair(jj, carry):
        step(2 * jj, s_even, s_odd, False)
        step(2 * jj + 1, s_odd, s_even, False)
        return carry

    lax.fori_loop(0, i // 2, pair, 0)

    @pl.when(i % 2 == 1)
    def _():
        step(i - 1, s_even, s_odd, False)
        step(i, s_odd, s_even, True)

    @pl.when(i % 2 == 0)
    def _():
        step(i, s_even, s_odd, True)

    outs = []
    for hh in range(2):
        acc = acc_scr[hh]
        outs.append(acc / acc[:, ONE_LANE_V:ONE_LANE_V + 1])
    lane = lax.broadcasted_iota(jnp.int32, (t, LANES), 1)
    o_ref[...] = jnp.where(lane < HEAD_DIM, outs[0], pltpu.roll(outs[1], HEAD_DIM, axis=1)).astype(bf16)


def _flash(qp, kp, vp):
    n = qp.shape[1]
    t = ATT_TILE
    resident = pl.BlockSpec((2, n, LANES), lambda p, i: (p, 0, 0))
    return pl.pallas_call(
        functools.partial(_flash_kernel, t=t),
        grid=(N_HEADS // 2, n // t),
        in_specs=[pl.BlockSpec((2, t, LANES), lambda p, i: (p, i, 0)), resident, resident],
        out_specs=pl.BlockSpec((t, LANES), lambda p, i: (i, p)),
        out_shape=jax.ShapeDtypeStruct((n, D_MODEL), bf16),
        scratch_shapes=[pltpu.VMEM((2, t, LANES), f32), pltpu.VMEM((2, t, LANES), f32),
                        pltpu.VMEM((2, t, t), f32), pltpu.VMEM((2, t, t), f32)],
        compiler_params=_params(2),
        name="flash",
    )(qp, kp, vp)


def _cum_kernel(past_ref, new_ref, o_ref, *, past_len, new_len, chunk):
    t = _tri(chunk)
    carry = jnp.zeros((1, N_HEADS), f32)
    for s in range(0, past_len, chunk):
        c = _prefix_sum(t, past_ref[0, s:s + chunk, :]) + carry
        o_ref[0, s:s + chunk, :] = c * LOG2E
        carry = c[chunk - 1:chunk, :]
    c = _prefix_sum(_tri(new_len), new_ref[0]) + carry
    o_ref[0, past_len:past_len + new_len, :] = c * LOG2E


def _cum(past_logf, new_logf):
    b, past_len, _ = past_logf.shape
    new_len = new_logf.shape[1]
    total = past_len + new_len
    return pl.pallas_call(
        functools.partial(_cum_kernel, past_len=past_len, new_len=new_len, chunk=512),
        grid=(b,),
        in_specs=[pl.BlockSpec((1, past_len, N_HEADS), lambda i: (i, 0, 0)),
                  pl.BlockSpec((1, new_len, N_HEADS), lambda i: (i, 0, 0))],
        out_specs=pl.BlockSpec((1, total, N_HEADS), lambda i: (i, 0, 0)),
        out_shape=jax.ShapeDtypeStruct((b, total, N_HEADS), f32),
        compiler_params=_params(1),
        name="cum",
    )(past_logf, new_logf)


def _expand_heads(e, x):
    hi, mid, lo = _split3(x)
    return _dot_nt(e, hi) + _dot_nt(e, mid) + _dot_nt(e, lo)


def _decode_attn_kernel(q_ref, kc_ref, vc_ref, kn_ref, vn_ref, c_ref, e_ref, o_ref, *, past_len, new_len):
    rows = N_HEADS * new_len
    q = q_ref[0]
    row_head = lax.broadcasted_iota(jnp.int32, (rows, D_MODEL), 0) // new_len
    col_head = lax.broadcasted_iota(jnp.int32, (rows, D_MODEL), 1) // HEAD_DIM
    qbd = jnp.where(row_head == col_head, jnp.tile(q, (N_HEADS, 1)), 0.0).astype(bf16)
    e = e_ref[...]
    c2 = c_ref[0]
    ck_new = _expand_heads(e, c2[past_len:past_len + new_len, :])
    t_of_row = lax.broadcasted_iota(jnp.int32, (rows, new_len), 0) % new_len
    s_idx = lax.broadcasted_iota(jnp.int32, (rows, new_len), 1)
    cq = jnp.sum(jnp.where(s_idx == t_of_row, ck_new, 0.0), axis=1, keepdims=True)

    lg = _dot_nt(qbd, kc_ref[0]) + cq - _expand_heads(e, c2[0:past_len, :])
    lg_new = jnp.where(s_idx <= t_of_row, _dot_nt(qbd, kn_ref[0].astype(bf16)) + cq - ck_new, NEG_BIG)
    m = jnp.maximum(jnp.max(lg, axis=1, keepdims=True), jnp.max(lg_new, axis=1, keepdims=True))
    p = jnp.exp2(lg - m)
    p_new = jnp.exp2(lg_new - m)
    denom = jnp.sum(p, axis=1, keepdims=True) + jnp.sum(p_new, axis=1, keepdims=True)
    o = (_dot(p.astype(bf16), vc_ref[0])
         + _dot(p_new.astype(bf16), vn_ref[0].astype(bf16))) / denom
    lane = lax.broadcasted_iota(jnp.int32, (new_len, LANES), 1)
    for pp in range(N_HEADS // 2):
        even = o[(2 * pp) * new_len:(2 * pp + 1) * new_len, pp * LANES:(pp + 1) * LANES]
        odd = o[(2 * pp + 1) * new_len:(2 * pp + 2) * new_len, pp * LANES:(pp + 1) * LANES]
        o_ref[0, :, pp * LANES:(pp + 1) * LANES] = jnp.where(lane < HEAD_DIM, even, odd).astype(bf16)


def _decode_attn(q, cache_k, cache_v, k_new, v_new, c2, expand):
    b, past_len, _ = cache_k.shape
    new_len = q.shape[1]
    cache = pl.BlockSpec((1, past_len, D_MODEL), lambda i: (i, 0, 0))
    new = pl.BlockSpec((1, new_len, D_MODEL), lambda i: (i, 0, 0))
    return pl.pallas_call(
        functools.partial(_decode_attn_kernel, past_len=past_len, new_len=new_len),
        grid=(b,),
        in_specs=[new, cache, cache, new, new,
                  pl.BlockSpec((1, past_len + new_len, N_HEADS), lambda i: (i, 0, 0)),
                  _resident((N_HEADS * new_len, N_HEADS))],
        out_specs=new,
        out_shape=jax.ShapeDtypeStruct((b, new_len, D_MODEL), bf16),
        compiler_params=_params(1),
        name="decode_attn",
    )(q, cache_k, cache_v, k_new, v_new, c2, expand)


def kernel(x_prompt, x_sample, cache_k, cache_v, cache_logf, state_conv, norm_mix, norm_mlp, conv_in, conv_w,
           conv_out, norm_kv, w_k, w_v, w_f, b_f, w_q, w_o, mlp_up, mlp_down, norm_final):
    assert x_prompt.shape[0] == 1 and x_prompt.shape[2] == D_MODEL
    seq = x_prompt.shape[1]
    dec_b, dec_s, _ = x_sample.shape
    past_len = cache_k.shape[1]
    assert seq % ROW_TILE == 0 and seq % ATT_TILE == 0 and (dec_b * dec_s) % 8 == 0

    pk, ck_const, pq, cq_const, expand = _placement_constants()
    row = lambda a: a.reshape(1, -1)
    conv_in_b, conv_out_b = conv_in.astype(bf16), conv_out.astype(bf16)
    w_k_b, w_v_b, w_q_b, w_o_b = w_k.astype(bf16), w_v.astype(bf16), w_q.astype(bf16), w_o.astype(bf16)
    up_b, down_b = mlp_up.astype(bf16), mlp_down.astype(bf16)
    w_f_b = jnp.pad(w_f, ((0, 0), (0, LANES - N_HEADS))).astype(bf16)
    b_f_p = jnp.pad(b_f, (0, LANES - N_HEADS)).reshape(1, LANES)

    def mlp(x, l, **kw):
        return _mlp(x, row(norm_mlp[l]), up_b[l], down_b[l], **kw)

    x = x_prompt.reshape(seq, D_MODEL)
    tails = []
    for l in range(N_A):
        x, tail = _mixer(x, row(norm_mix[l]), conv_in_b[l], conv_w[l], conv_out_b[l])
        tails.append(tail[6:8])
        x = mlp(x, l)
    k_p, v_p, logf_p, c2_p, kp, vp = _kv(x, row(norm_kv), w_k_b, w_v_b, w_f_b, b_f_p, pk, ck_const)
    for l in range(N_A, DEPTH):
        j = l - N_A
        qp = _qproj(x, row(norm_mix[l]), w_q_b[j], c2_p, pq, cq_const)
        attn = _flash(qp, kp, vp)
        x = mlp(x, l, attn=attn, w_o=w_o_b[j], g_final=row(norm_final) if l == DEPTH - 1 else None)
    y_prompt = x.reshape(1, seq, D_MODEL)
    conv_prompt = jnp.stack(tails).reshape(N_A, 1, 2, D_MODEL)

    n_s = dec_b * dec_s
    x = x_sample.reshape(n_s, D_MODEL)
    new_states = []
    for l in range(N_A):
        st = state_conv[l]
        pad = jnp.zeros((dec_b, dec_s - 2, D_MODEL), f32)
        s2 = jnp.concatenate([st, pad], axis=1).reshape(n_s, D_MODEL)
        s1 = jnp.concatenate([st[:, 1:2], pad, pad[:, 0:1]], axis=1).reshape(n_s, D_MODEL)
        x, u = _mixer(x, row(norm_mix[l]), conv_in_b[l], conv_w[l], conv_out_b[l],
                      state_rows=(s1, s2), seq_len=dec_s)
        new_states.append(u.reshape(dec_b, dec_s, D_MODEL)[:, dec_s - 2:])
        x = mlp(x, l)
    k_s, v_s, logf_s = _kv(x, row(norm_kv), w_k_b, w_v_b, w_f_b, b_f_p)
    c2_s = _cum(cache_logf, logf_s.reshape(dec_b, dec_s, N_HEADS))
    k_s3, v_s3 = k_s.reshape(dec_b, dec_s, D_MODEL), v_s.reshape(dec_b, dec_s, D_MODEL)
    cache_k3 = cache_k.reshape(dec_b, past_len, D_MODEL).astype(bf16)
    cache_v3 = cache_v.reshape(dec_b, past_len, D_MODEL).astype(bf16)
    for l in range(N_A, DEPTH):
        j = l - N_A
        q = _qproj(x, row(norm_mix[l]), w_q_b[j])
        attn = _decode_attn(q.reshape(dec_b, dec_s, D_MODEL), cache_k3, cache_v3, k_s3, v_s3, c2_s, expand)
        x = mlp(x, l, attn=attn.reshape(n_s, D_MODEL), w_o=w_o_b[j],
                g_final=row(norm_final) if l == DEPTH - 1 else None)
    y_sample = x.reshape(dec_b, dec_s, D_MODEL)

    return (y_prompt, y_sample,
            k_p.reshape(1, seq, N_HEADS, HEAD_DIM), v_p.reshape(1, seq, N_HEADS, HEAD_DIM),
            logf_p.reshape(1, seq, N_HEADS), conv_prompt,
            k_s.reshape(dec_b, dec_s, N_HEADS, HEAD_DIM), v_s.reshape(dec_b, dec_s, N_HEADS, HEAD_DIM),
            logf_s.reshape(dec_b, dec_s, N_HEADS), jnp.stack(new_states))
```

```python
import functools
import math

import numpy as np
import jax
import jax.numpy as jnp
from jax import lax
from jax.experimental import pallas as pl
from jax.experimental.pallas import tpu as pltpu

D_MODEL = 1024
N_HEADS = 16
HEAD_DIM = 64
D_FF = 4 * D_MODEL
N_A = 2
DEPTH = 4
RMS_EPS = 1e-6
LOG2E = math.log2(math.e)
Q_SCALE = HEAD_DIM ** -0.5 * LOG2E
NEG_BIG = -1e30

LANES = 128
ROW_TILE = 512
ATT_TILE = 512
VMEM_LIMIT = 56 * 1024 * 1024

CQ_LANE0 = 64
CK_LANE0 = 112
VT_ROWS = 80

bf16 = jnp.bfloat16
f32 = jnp.float32


def _dot(a, b):
    return jnp.dot(a, b, preferred_element_type=f32)


def _dot_nt(a, b):
    return lax.dot_general(a, b, (((1,), (1,)), ((), ())), preferred_element_type=f32)


def _rms(x, g):
    ms = jnp.mean(x * x, axis=-1, keepdims=True)
    return x * lax.rsqrt(ms + RMS_EPS) * g


def _split3(x):
    hi = x.astype(bf16)
    r1 = x - hi.astype(f32)
    mid = r1.astype(bf16)
    lo = (r1 - mid.astype(f32)).astype(bf16)
    return hi, mid, lo


def _tri(n):
    r = lax.broadcasted_iota(jnp.int32, (n, n), 0)
    c = lax.broadcasted_iota(jnp.int32, (n, n), 1)
    return jnp.where(c <= r, 1.0, 0.0).astype(bf16)


def _prefix_sum(t, x):
    hi, mid, lo = _split3(x)
    return _dot(t, hi) + _dot(t, mid) + _dot(t, lo)


def _log_sigmoid(z):
    return jnp.minimum(z, 0.0) - jnp.log1p(jnp.exp(-jnp.abs(z)))


def _resident(shape):
    nd = len(shape)
    return pl.BlockSpec(shape, lambda *_: (0,) * nd, pipeline_mode=pl.Buffered(1))


def _params(n_axes):
    return pltpu.CompilerParams(dimension_semantics=("arbitrary",) * n_axes,
                                vmem_limit_bytes=VMEM_LIMIT)


def _placement_constants():
    pk = np.zeros((3, LANES, N_HEADS * LANES), np.float32)
    ck_const = np.zeros((1, N_HEADS * LANES), np.float32)
    pq = np.zeros((3, LANES, LANES), np.float32)
    cq_const = np.zeros((1, LANES), np.float32)
    for j in range(3):
        cq_const[0, CK_LANE0 + j] = 1.0
        for h in range(N_HEADS):
            pk[j, h, h * LANES + CK_LANE0 + j] = -1.0
            ck_const[0, h * LANES + CQ_LANE0 + 16 * j + h] = 1.0
            pq[j, h, CQ_LANE0 + 16 * j + h] = 1.0
    expand = np.zeros((N_HEADS * 16, N_HEADS), np.float32)
    for h in range(N_HEADS):
        expand[h * 16:(h + 1) * 16, h] = 1.0
    return (jnp.asarray(pk, bf16), jnp.asarray(ck_const), jnp.asarray(pq, bf16),
            jnp.asarray(cq_const), jnp.asarray(expand, bf16))


def _pack_heads(x, extras_of_head, out_ref):
    lane = lax.broadcasted_iota(jnp.int32, (x.shape[0], LANES), 1)
    for h in range(N_HEADS):
        pair = x[:, (h // 2) * LANES:(h // 2 + 1) * LANES]
        base = pair if h % 2 == 0 else pltpu.roll(pair, HEAD_DIM, axis=1)
        out_ref[h] = jnp.where(lane < HEAD_DIM, base, extras_of_head(h)).astype(bf16)


def _mixer_kernel(*refs, tm, carry_rows, seq_len):
    if carry_rows:
        x_ref, g_ref, win_ref, cw_ref, wout_ref, o_ref, u_ref, u_scr = refs
    else:
        x_ref, g_ref, win_ref, cw_ref, wout_ref, s1_ref, s2_ref, o_ref, u_ref, u_scr = refs

    @pl.when(pl.program_id(0) == 0)
    def _():
        u_scr[0:8, :] = jnp.zeros((8, D_MODEL), f32)

    x = x_ref[...]
    h = _rms(x, g_ref[...]).astype(bf16)
    bcx = _dot(h, win_ref[...])
    gb = bcx[:, 0:D_MODEL]
    u = bcx[:, D_MODEL:2 * D_MODEL] * bcx[:, 2 * D_MODEL:3 * D_MODEL]
    u_scr[8:8 + tm, :] = u
    u1 = u_scr[7:7 + tm, :]
    u2 = u_scr[6:6 + tm, :]
    if carry_rows:
        tail = u_scr[tm:tm + 8, :]
        u_scr[0:8, :] = tail
        u_ref[...] = tail
    else:
        pos = lax.broadcasted_iota(jnp.int32, (tm, 1), 0) % seq_len
        u1 = jnp.where(pos < 1, s1_ref[...], u1)
        u2 = jnp.where(pos < 2, s2_ref[...], u2)
        u_ref[...] = u
    cw = cw_ref[...]
    conv = cw[0:1, :] * u2 + cw[1:2, :] * u1 + cw[2:3, :] * u
    o_ref[...] = x + _dot((gb * conv).astype(bf16), wout_ref[...])


def _mixer(x, g, w_in, cw, w_out, state_rows=None, seq_len=None):
    n = x.shape[0]
    tm = min(ROW_TILE, n)
    carry = state_rows is None
    row = pl.BlockSpec((tm, D_MODEL), lambda i: (i, 0))
    in_specs = [row, _resident((1, D_MODEL)), _resident((D_MODEL, 3 * D_MODEL)),
                _resident((3, D_MODEL)), _resident((D_MODEL, D_MODEL))]
    args = [x, g, w_in, cw, w_out]
    if carry:
        u_spec = pl.BlockSpec((8, D_MODEL), lambda i: (0, 0))
        u_shape = jax.ShapeDtypeStruct((8, D_MODEL), f32)
    else:
        assert n == tm
        in_specs += [row, row]
        args += list(state_rows)
        u_spec = row
        u_shape = jax.ShapeDtypeStruct((n, D_MODEL), f32)
    return pl.pallas_call(
        functools.partial(_mixer_kernel, tm=tm, carry_rows=carry, seq_len=seq_len),
        grid=(n // tm,),
        in_specs=in_specs,
        out_specs=[row, u_spec],
        out_shape=[jax.ShapeDtypeStruct((n, D_MODEL), f32), u_shape],
        scratch_shapes=[pltpu.VMEM((tm + 8, D_MODEL), f32)],
        compiler_params=_params(1),
        name="mixer",
    )(*args)


def _mlp_kernel(*refs, has_attn, final_norm):
    refs = list(refs)
    x_ref = refs.pop(0)
    if has_attn:
        attn_ref = refs.pop(0)
        wo_ref = refs.pop(0)
    g_ref, wup_ref, wdn_ref = refs[0:3]
    gf_ref = refs[3] if final_norm else None
    o_ref = refs[-1]

    x = x_ref[...]
    if has_attn:
        x = x + _dot(attn_ref[...], wo_ref[...])
    h = _rms(x, g_ref[...]).astype(bf16)
    a = jnp.maximum(_dot(h, wup_ref[...]), 0.0)
    y = x + _dot((a * a).astype(bf16), wdn_ref[...])
    if final_norm:
        y = _rms(y, gf_ref[...])
    o_ref[...] = y


def _mlp(x, g, w_up, w_down, attn=None, w_o=None, g_final=None):
    n = x.shape[0]
    tm = min(ROW_TILE, n)
    row = pl.BlockSpec((tm, D_MODEL), lambda i: (i, 0))
    in_specs, args = [row], [x]
    if attn is not None:
        in_specs += [row, _resident((D_MODEL, D_MODEL))]
        args += [attn, w_o]
    in_specs += [_resident((1, D_MODEL)), _resident((D_MODEL, D_FF)), _resident((D_FF, D_MODEL))]
    args += [g, w_up, w_down]
    if g_final is not None:
        in_specs.append(_resident((1, D_MODEL)))
        args.append(g_final)
    return pl.pallas_call(
        functools.partial(_mlp_kernel, has_attn=attn is not None, final_norm=g_final is not None),
        grid=(n // tm,),
        in_specs=in_specs,
        out_specs=row,
        out_shape=jax.ShapeDtypeStruct((n, D_MODEL), f32),
        compiler_params=_params(1),
        name="mlp",
    )(*args)


def _kv_kernel(*refs, tm, pack):
    if pack:
        (x_ref, g_ref, wk_ref, wv_ref, wf_ref, bf_ref, pk_ref, ckc_ref,
         k_ref, v_ref, lf_ref, c_ref, kp_ref, vt_ref, carry) = refs
    else:
        x_ref, g_ref, wk_ref, wv_ref, wf_ref, bf_ref, k_ref, v_ref, lf_ref = refs

    hk = _rms(x_ref[...], g_ref[...]).astype(bf16)
    k = _dot(hk, wk_ref[...])
    v = _dot(hk, wv_ref[...])
    z = _dot(hk, wf_ref[...]) + bf_ref[...]
    lane = lax.broadcasted_iota(jnp.int32, (tm, LANES), 1)
    logf = jnp.where(lane < N_HEADS, _log_sigmoid(z), 0.0)
    k_ref[...] = k
    v_ref[...] = v
    lf_ref[...] = logf[:, 0:N_HEADS]
    if not pack:
        return

    @pl.when(pl.program_id(0) == 0)
    def _():
        carry[...] = jnp.zeros_like(carry)

    c = _prefix_sum(_tri(tm), logf) + carry[0:1, :]
    carry[0:1, :] = c[tm - 1:tm, :]
    c2 = c * LOG2E
    c_ref[...] = c2
    hi, mid, lo = _split3(c2)
    extras = _dot(hi, pk_ref[0]) + _dot(mid, pk_ref[1]) + _dot(lo, pk_ref[2]) + ckc_ref[...]
    _pack_heads(k, lambda h: extras[:, h * LANES:(h + 1) * LANES], kp_ref)
    ones = jnp.where(lane == HEAD_DIM, 1.0, 0.0)
    for h in range(N_HEADS):
        pair = v[:, (h // 2) * LANES:(h // 2 + 1) * LANES]
        base = pair if h % 2 == 0 else pltpu.roll(pair, HEAD_DIM, axis=1)
        vt = jnp.transpose(jnp.where(lane < HEAD_DIM, base, ones))
        vt_ref[h, 0] = vt[0:VT_ROWS, :].astype(bf16)


def _kv(x, g, w_k, w_v, w_f, b_f, pk=None, ck_const=None):
    n = x.shape[0]
    tm = min(ROW_TILE, n)
    pack = pk is not None
    row = pl.BlockSpec((tm, D_MODEL), lambda i: (i, 0))
    heads = pl.BlockSpec((N_HEADS, tm, LANES), lambda i: (0, i, 0))
    in_specs = [row, _resident((1, D_MODEL)), _resident((D_MODEL, D_MODEL)), _resident((D_MODEL, D_MODEL)),
                _resident((D_MODEL, LANES)), _resident((1, LANES))]
    args = [x, g, w_k, w_v, w_f, b_f]
    out_specs = [row, row, pl.BlockSpec((tm, N_HEADS), lambda i: (i, 0))]
    out_shape = [jax.ShapeDtypeStruct((n, D_MODEL), f32), jax.ShapeDtypeStruct((n, D_MODEL), f32),
                 jax.ShapeDtypeStruct((n, N_HEADS), f32)]
    scratch = []
    if pack:
        in_specs += [_resident((3, LANES, N_HEADS * LANES)), _resident((1, N_HEADS * LANES))]
        args += [pk, ck_const]
        out_specs += [pl.BlockSpec((tm, LANES), lambda i: (i, 0)), heads,
                      pl.BlockSpec((N_HEADS, 1, VT_ROWS, tm), lambda i: (0, i, 0, 0))]
        out_shape += [jax.ShapeDtypeStruct((n, LANES), f32),
                      jax.ShapeDtypeStruct((N_HEADS, n, LANES), bf16),
                      jax.ShapeDtypeStruct((N_HEADS, n // tm, VT_ROWS, tm), bf16)]
        scratch = [pltpu.VMEM((8, LANES), f32)]
    return pl.pallas_call(
        functools.partial(_kv_kernel, tm=tm, pack=pack),
        grid=(n // tm,),
        in_specs=in_specs,
        out_specs=out_specs,
        out_shape=out_shape,
        scratch_shapes=scratch,
        compiler_params=_params(1),
        name="kv",
    )(*args)


def _qproj_kernel(*refs, pack):
    if pack:
        x_ref, g_ref, wq_ref, c_ref, pq_ref, cqc_ref, o_ref = refs
    else:
        x_ref, g_ref, wq_ref, o_ref = refs
    h = _rms(x_ref[...], g_ref[...]).astype(bf16)
    q = _dot(h, wq_ref[...]) * Q_SCALE
    if not pack:
        o_ref[...] = q
        return
    hi, mid, lo = _split3(c_ref[...])
    extras = _dot(hi, pq_ref[0]) + _dot(mid, pq_ref[1]) + _dot(lo, pq_ref[2]) + cqc_ref[...]
    _pack_heads(q, lambda h: extras, o_ref)


def _qproj(x, g, w_q, c2=None, pq=None, cq_const=None):
    n = x.shape[0]
    tm = min(ROW_TILE, n)
    pack = c2 is not None
    row = pl.BlockSpec((tm, D_MODEL), lambda i: (i, 0))
    in_specs = [row, _resident((1, D_MODEL)), _resident((D_MODEL, D_MODEL))]
    args = [x, g, w_q]
    if pack:
        in_specs += [pl.BlockSpec((tm, LANES), lambda i: (i, 0)), _resident((3, LANES, LANES)),
                     _resident((1, LANES))]
        args += [c2, pq, cq_const]
        out_spec = pl.BlockSpec((N_HEADS, tm, LANES), lambda i: (0, i, 0))
        out_shape = jax.ShapeDtypeStruct((N_HEADS, n, LANES), bf16)
    else:
        out_spec = row
        out_shape = jax.ShapeDtypeStruct((n, D_MODEL), f32)
    return pl.pallas_call(
        functools.partial(_qproj_kernel, pack=pack),
        grid=(n // tm,),
        in_specs=in_specs,
        out_specs=out_spec,
        out_shape=out_shape,
        compiler_params=_params(1),
        name="qproj",
    )(*args)


def _flash_kernel(q_ref, k_ref, vt_ref, o_ref, m_scr, acc_scr, s_even, s_odd, *, t):
    i = pl.program_id(1)
    for hh in range(2):
        m_scr[hh] = jnp.full((1, t), NEG_BIG, f32)
        acc_scr[hh] = jnp.zeros((VT_ROWS, t), f32)

    def logits(j, hh):
        start = pl.multiple_of(j * t, t)
        return _dot_nt(k_ref[hh, pl.ds(start, t), :], q_ref[hh])

    def step(j, s_cur, s_next, diagonal):
        for hh in range(2):
            s = s_cur[hh]
            if diagonal:
                key = lax.broadcasted_iota(jnp.int32, (t, t), 0)
                qry = lax.broadcasted_iota(jnp.int32, (t, t), 1)
                s = jnp.where(key <= qry, s, NEG_BIG)
            else:
                s_next[hh] = logits(j + 1, hh)
            m_prev = m_scr[hh]
            m_new = jnp.maximum(m_prev, jnp.max(s, axis=0, keepdims=True))
            p = jnp.exp2(s - m_new)
            pv = _dot(vt_ref[hh, j], p.astype(bf16))
            acc_scr[hh] = jnp.exp2(m_prev - m_new) * acc_scr[hh] + pv
            m_scr[hh] = m_new

    for hh in range(2):
        s_even[hh] = logits(0, hh)

    def pair(jj, carry):
        step(2 * jj, s_even, s_odd, False)
        step(2 * jj + 1, s_odd, s_even, False)
        return carry

    lax.fori_loop(0, i // 2, pair, 0)

    @pl.when(i % 2 == 1)
    def _():
        step(i - 1, s_even, s_odd, False)
        step(i, s_odd, s_even, True)

    @pl.when(i % 2 == 0)
    def _():
        step(i, s_even, s_odd, True)

    outs = []
    for hh in range(2):
        acc = acc_scr[hh]
        outs.append(acc[0:HEAD_DIM, :] / acc[HEAD_DIM:HEAD_DIM + 1, :])
    o_ref[...] = jnp.transpose(jnp.concatenate(outs, axis=0)).astype(bf16)


def _flash(qp, kp, vt):
    n = qp.shape[1]
    t = ATT_TILE
    assert vt.shape == (N_HEADS, n // t, VT_ROWS, t)
    return pl.pallas_call(
        functools.partial(_flash_kernel, t=t),
        grid=(N_HEADS // 2, n // t),
        in_specs=[pl.BlockSpec((2, t, LANES), lambda p, i: (p, i, 0)),
                  pl.BlockSpec((2, n, LANES), lambda p, i: (p, 0, 0)),
                  pl.BlockSpec((2, n // t, VT_ROWS, t), lambda p, i: (p, 0, 0, 0))],
        out_specs=pl.BlockSpec((t, LANES), lambda p, i: (i, p)),
        out_shape=jax.ShapeDtypeStruct((n, D_MODEL), bf16),
        scratch_shapes=[pltpu.VMEM((2, 1, t), f32), pltpu.VMEM((2, VT_ROWS, t), f32),
                        pltpu.VMEM((2, t, t), f32), pltpu.VMEM((2, t, t), f32)],
        compiler_params=_params(2),
        name="flash",
    )(qp, kp, vt)


def _cum_kernel(past_ref, new_ref, o_ref, *, past_len, new_len, chunk):
    t = _tri(chunk)
    carry = jnp.zeros((1, N_HEADS), f32)
    for s in range(0, past_len, chunk):
        c = _prefix_sum(t, past_ref[0, s:s + chunk, :]) + carry
        o_ref[0, s:s + chunk, :] = c * LOG2E
        carry = c[chunk - 1:chunk, :]
    c = _prefix_sum(_tri(new_len), new_ref[0]) + carry
    o_ref[0, past_len:past_len + new_len, :] = c * LOG2E


def _cum(past_logf, new_logf):
    b, past_len, _ = past_logf.shape
    new_len = new_logf.shape[1]
    total = past_len + new_len
    return pl.pallas_call(
        functools.partial(_cum_kernel, past_len=past_len, new_len=new_len, chunk=512),
        grid=(b,),
        in_specs=[pl.BlockSpec((1, past_len, N_HEADS), lambda i: (i, 0, 0)),
                  pl.BlockSpec((1, new_len, N_HEADS), lambda i: (i, 0, 0))],
        out_specs=pl.BlockSpec((1, total, N_HEADS), lambda i: (i, 0, 0)),
        out_shape=jax.ShapeDtypeStruct((b, total, N_HEADS), f32),
        compiler_params=_params(1),
        name="cum",
    )(past_logf, new_logf)


def _expand_heads(e, x):
    hi, mid, lo = _split3(x)
    return _dot_nt(e, hi) + _dot_nt(e, mid) + _dot_nt(e, lo)


def _decode_attn_kernel(q_ref, kc_ref, vc_ref, kn_ref, vn_ref, c_ref, e_ref, o_ref, *, past_len, new_len):
    rows = N_HEADS * new_len
    q = q_ref[0]
    row_head = lax.broadcasted_iota(jnp.int32, (rows, D_MODEL), 0) // new_len
    col_head = lax.broadcasted_iota(jnp.int32, (rows, D_MODEL), 1) // HEAD_DIM
    qbd = jnp.where(row_head == col_head, jnp.tile(q, (N_HEADS, 1)), 0.0).astype(bf16)
    e = e_ref[...]
    c2 = c_ref[0]
    ck_new = _expand_heads(e, c2[past_len:past_len + new_len, :])
    t_of_row = lax.broadcasted_iota(jnp.int32, (rows, new_len), 0) % new_len
    s_idx = lax.broadcasted_iota(jnp.int32, (rows, new_len), 1)
    cq = jnp.sum(jnp.where(s_idx == t_of_row, ck_new, 0.0), axis=1, keepdims=True)

    lg = _dot_nt(qbd, kc_ref[0]) + cq - _expand_heads(e, c2[0:past_len, :])
    lg_new = jnp.where(s_idx <= t_of_row, _dot_nt(qbd, kn_ref[0].astype(bf16)) + cq - ck_new, NEG_BIG)
    m = jnp.maximum(jnp.max(lg, axis=1, keepdims=True), jnp.max(lg_new, axis=1, keepdims=True))
    p = jnp.exp2(lg - m)
    p_new = jnp.exp2(lg_new - m)
    denom = jnp.sum(p, axis=1, keepdims=True) + jnp.sum(p_new, axis=1, keepdims=True)
    o = (_dot(p.astype(bf16), vc_ref[0])
         + _dot(p_new.astype(bf16), vn_ref[0].astype(bf16))) / denom
    lane = lax.broadcasted_iota(jnp.int32, (new_len, LANES), 1)
    for pp in range(N_HEADS // 2):
        even = o[(2 * pp) * new_len:(2 * pp + 1) * new_len, pp * LANES:(pp + 1) * LANES]
        odd = o[(2 * pp + 1) * new_len:(2 * pp + 2) * new_len, pp * LANES:(pp + 1) * LANES]
        o_ref[0, :, pp * LANES:(pp + 1) * LANES] = jnp.where(lane < HEAD_DIM, even, odd).astype(bf16)


def _decode_attn(q, cache_k, cache_v, k_new, v_new, c2, expand):
    b, past_len, _ = cache_k.shape
    new_len = q.shape[1]
    cache = pl.BlockSpec((1, past_len, D_MODEL), lambda i: (i, 0, 0))
    new = pl.BlockSpec((1, new_len, D_MODEL), lambda i: (i, 0, 0))
    return pl.pallas_call(
        functools.partial(_decode_attn_kernel, past_len=past_len, new_len=new_len),
        grid=(b,),
        in_specs=[new, cache, cache, new, new,
                  pl.BlockSpec((1, past_len + new_len, N_HEADS), lambda i: (i, 0, 0)),
                  _resident((N_HEADS * new_len, N_HEADS))],
        out_specs=new,
        out_shape=jax.ShapeDtypeStruct((b, new_len, D_MODEL), bf16),
        compiler_params=_params(1),
        name="decode_attn",
    )(q, cache_k, cache_v, k_new, v_new, c2, expand)


def kernel(x_prompt, x_sample, cache_k, cache_v, cache_logf, state_conv, norm_mix, norm_mlp, conv_in, conv_w,
           conv_out, norm_kv, w_k, w_v, w_f, b_f, w_q, w_o, mlp_up, mlp_down, norm_final):
    assert x_prompt.shape[0] == 1 and x_prompt.shape[2] == D_MODEL
    seq = x_prompt.shape[1]
    dec_b, dec_s, _ = x_sample.shape
    past_len = cache_k.shape[1]
    assert seq % ROW_TILE == 0 and ROW_TILE == ATT_TILE and (dec_b * dec_s) % 8 == 0

    pk, ck_const, pq, cq_const, expand = _placement_constants()
    row = lambda a: a.reshape(1, -1)
    conv_in_b, conv_out_b = conv_in.astype(bf16), conv_out.astype(bf16)
    w_k_b, w_v_b, w_q_b, w_o_b = w_k.astype(bf16), w_v.astype(bf16), w_q.astype(bf16), w_o.astype(bf16)
    up_b, down_b = mlp_up.astype(bf16), mlp_down.astype(bf16)
    w_f_b = jnp.pad(w_f, ((0, 0), (0, LANES - N_HEADS))).astype(bf16)
    b_f_p = jnp.pad(b_f, (0, LANES - N_HEADS)).reshape(1, LANES)

    def mlp(x, l, **kw):
        return _mlp(x, row(norm_mlp[l]), up_b[l], down_b[l], **kw)

    x = x_prompt.reshape(seq, D_MODEL)
    tails = []
    for l in range(N_A):
        x, tail = _mixer(x, row(norm_mix[l]), conv_in_b[l], conv_w[l], conv_out_b[l])
        tails.append(tail[6:8])
        x = mlp(x, l)
    k_p, v_p, logf_p, c2_p, kp, vt = _kv(x, row(norm_kv), w_k_b, w_v_b, w_f_b, b_f_p, pk, ck_const)
    for l in range(N_A, DEPTH):
        j = l - N_A
        qp = _qproj(x, row(norm_mix[l]), w_q_b[j], c2_p, pq, cq_const)
        attn = _flash(qp, kp, vt)
        x = mlp(x, l, attn=attn, w_o=w_o_b[j], g_final=row(norm_final) if l == DEPTH - 1 else None)
    y_prompt = x.reshape(1, seq, D_MODEL)
    conv_prompt = jnp.stack(tails).reshape(N_A, 1, 2, D_MODEL)

    n_s = dec_b * dec_s
    x = x_sample.reshape(n_s, D_MODEL)
    new_states = []
    for l in range(N_A):
        st = state_conv[l]
        pad = jnp.zeros((dec_b, dec_s - 2, D_MODEL), f32)
        s2 = jnp.concatenate([st, pad], axis=1).reshape(n_s, D_MODEL)
        s1 = jnp.concatenate([st[:, 1:2], pad, pad[:, 0:1]], axis=1).reshape(n_s, D_MODEL)
        x, u = _mixer(x, row(norm_mix[l]), conv_in_b[l], conv_w[l], conv_out_b[l],
                      state_rows=(s1, s2), seq_len=dec_s)
        new_states.append(u.reshape(dec_b, dec_s, D_MODEL)[:, dec_s - 2:])
        x = mlp(x, l)
    k_s, v_s, logf_s = _kv(x, row(norm_kv), w_k_b, w_v_b, w_f_b, b_f_p)
    c2_s = _cum(cache_logf, logf_s.reshape(dec_b, dec_s, N_HEADS))
    k_s3, v_s3 = k_s.reshape(dec_b, dec_s, D_MODEL), v_s.reshape(dec_b, dec_s, D_MODEL)
    cache_k3 = cache_k.reshape(dec_b, past_len, D_MODEL).astype(bf16)
    cache_v3 = cache_v.reshape(dec_b, past_len, D_MODEL).astype(bf16)
    for l in range(N_A, DEPTH):
        j = l - N_A
        q = _qproj(x, row(norm_mix[l]), w_q_b[j])
        attn = _decode_attn(q.reshape(dec_b, dec_s, D_MODEL), cache_k3, cache_v3, k_s3, v_s3, c2_s, expand)
        x = mlp(x, l, attn=attn.reshape(n_s, D_MODEL), w_o=w_o_b[j],
                g_final=row(norm_final) if l == DEPTH - 1 else None)
    y_sample = x.reshape(dec_b, dec_s, D_MODEL)

    return (y_prompt, y_sample,
            k_p.reshape(1, seq, N_HEADS, HEAD_DIM), v_p.reshape(1, seq, N_HEADS, HEAD_DIM),
            logf_p.reshape(1, seq, N_HEADS), conv_prompt,
            k_s.reshape(dec_b, dec_s, N_HEADS, HEAD_DIM), v_s.reshape(dec_b, dec_s, N_HEADS, HEAD_DIM),
            logf_s.reshape(dec_b, dec_s, N_HEADS), jnp.stack(new_states))
```

```python
import functools
import math

import numpy as np
import jax
import jax.numpy as jnp
from jax import lax
from jax.experimental import pallas as pl
from jax.experimental.pallas import tpu as pltpu

D_MODEL = 1024
N_HEADS = 16
HEAD_DIM = 64
D_FF = 4 * D_MODEL
N_A = 2
DEPTH = 4
RMS_EPS = 1e-6
LOG2E = math.log2(math.e)
Q_SCALE = HEAD_DIM ** -0.5 * LOG2E
NEG_BIG = -1e30

LANES = 128
ROW_TILE = 512
ATT_TILE = 512
QUERY_CHUNK = 256
VMEM_LIMIT = 56 * 1024 * 1024

CQ_LANE0 = 64
CK_LANE0 = 112
VT_ROWS = 80

bf16 = jnp.bfloat16
f32 = jnp.float32


def _dot(a, b):
    return jnp.dot(a, b, preferred_element_type=f32)


def _dot_nt(a, b):
    return lax.dot_general(a, b, (((1,), (1,)), ((), ())), preferred_element_type=f32)


def _rms(x, g):
    ms = jnp.mean(x * x, axis=-1, keepdims=True)
    return x * lax.rsqrt(ms + RMS_EPS) * g


def _split3(x):
    hi = x.astype(bf16)
    r1 = x - hi.astype(f32)
    mid = r1.astype(bf16)
    lo = (r1 - mid.astype(f32)).astype(bf16)
    return hi, mid, lo


def _tri(n):
    r = lax.broadcasted_iota(jnp.int32, (n, n), 0)
    c = lax.broadcasted_iota(jnp.int32, (n, n), 1)
    return jnp.where(c <= r, 1.0, 0.0).astype(bf16)


def _prefix_sum(t, x):
    hi, mid, lo = _split3(x)
    return _dot(t, hi) + _dot(t, mid) + _dot(t, lo)


def _log_sigmoid(z):
    return jnp.minimum(z, 0.0) - jnp.log1p(jnp.exp(-jnp.abs(z)))


def _resident(shape):
    nd = len(shape)
    return pl.BlockSpec(shape, lambda *_: (0,) * nd, pipeline_mode=pl.Buffered(1))


def _params(n_axes):
    return pltpu.CompilerParams(dimension_semantics=("arbitrary",) * n_axes,
                                vmem_limit_bytes=VMEM_LIMIT)


def _placement_constants():
    pk = np.zeros((3, LANES, N_HEADS * LANES), np.float32)
    ck_const = np.zeros((1, N_HEADS * LANES), np.float32)
    pq = np.zeros((3, LANES, LANES), np.float32)
    cq_const = np.zeros((1, LANES), np.float32)
    for j in range(3):
        cq_const[0, CK_LANE0 + j] = 1.0
        for h in range(N_HEADS):
            pk[j, h, h * LANES + CK_LANE0 + j] = -1.0
            ck_const[0, h * LANES + CQ_LANE0 + 16 * j + h] = 1.0
            pq[j, h, CQ_LANE0 + 16 * j + h] = 1.0
    expand = np.zeros((N_HEADS * 16, N_HEADS), np.float32)
    for h in range(N_HEADS):
        expand[h * 16:(h + 1) * 16, h] = 1.0
    return (jnp.asarray(pk, bf16), jnp.asarray(ck_const), jnp.asarray(pq, bf16),
            jnp.asarray(cq_const), jnp.asarray(expand, bf16))


def _pack_heads(x, extras_of_head, out_ref):
    lane = lax.broadcasted_iota(jnp.int32, (x.shape[0], LANES), 1)
    for h in range(N_HEADS):
        pair = x[:, (h // 2) * LANES:(h // 2 + 1) * LANES]
        base = pair if h % 2 == 0 else pltpu.roll(pair, HEAD_DIM, axis=1)
        out_ref[h] = jnp.where(lane < HEAD_DIM, base, extras_of_head(h)).astype(bf16)


def _mixer_kernel(*refs, tm, carry_rows, seq_len):
    if carry_rows:
        x_ref, g_ref, win_ref, cw_ref, wout_ref, o_ref, u_ref, u_scr = refs
    else:
        x_ref, g_ref, win_ref, cw_ref, wout_ref, s1_ref, s2_ref, o_ref, u_ref, u_scr = refs

    @pl.when(pl.program_id(0) == 0)
    def _():
        u_scr[0:8, :] = jnp.zeros((8, D_MODEL), f32)

    x = x_ref[...]
    h = _rms(x, g_ref[...]).astype(bf16)
    bcx = _dot(h, win_ref[...])
    gb = bcx[:, 0:D_MODEL]
    u = bcx[:, D_MODEL:2 * D_MODEL] * bcx[:, 2 * D_MODEL:3 * D_MODEL]
    u_scr[8:8 + tm, :] = u
    u1 = u_scr[7:7 + tm, :]
    u2 = u_scr[6:6 + tm, :]
    if carry_rows:
        tail = u_scr[tm:tm + 8, :]
        u_scr[0:8, :] = tail
        u_ref[...] = tail
    else:
        pos = lax.broadcasted_iota(jnp.int32, (tm, 1), 0) % seq_len
        u1 = jnp.where(pos < 1, s1_ref[...], u1)
        u2 = jnp.where(pos < 2, s2_ref[...], u2)
        u_ref[...] = u
    cw = cw_ref[...]
    conv = cw[0:1, :] * u2 + cw[1:2, :] * u1 + cw[2:3, :] * u
    o_ref[...] = x + _dot((gb * conv).astype(bf16), wout_ref[...])


def _mixer(x, g, w_in, cw, w_out, state_rows=None, seq_len=None):
    n = x.shape[0]
    tm = min(ROW_TILE, n)
    carry = state_rows is None
    row = pl.BlockSpec((tm, D_MODEL), lambda i: (i, 0))
    in_specs = [row, _resident((1, D_MODEL)), _resident((D_MODEL, 3 * D_MODEL)),
                _resident((3, D_MODEL)), _resident((D_MODEL, D_MODEL))]
    args = [x, g, w_in, cw, w_out]
    if carry:
        u_spec = pl.BlockSpec((8, D_MODEL), lambda i: (0, 0))
        u_shape = jax.ShapeDtypeStruct((8, D_MODEL), f32)
    else:
        assert n == tm
        in_specs += [row, row]
        args += list(state_rows)
        u_spec = row
        u_shape = jax.ShapeDtypeStruct((n, D_MODEL), f32)
    return pl.pallas_call(
        functools.partial(_mixer_kernel, tm=tm, carry_rows=carry, seq_len=seq_len),
        grid=(n // tm,),
        in_specs=in_specs,
        out_specs=[row, u_spec],
        out_shape=[jax.ShapeDtypeStruct((n, D_MODEL), f32), u_shape],
        scratch_shapes=[pltpu.VMEM((tm + 8, D_MODEL), f32)],
        compiler_params=_params(1),
        name="mixer",
    )(*args)


def _mlp_kernel(*refs, has_attn, final_norm):
    refs = list(refs)
    x_ref = refs.pop(0)
    if has_attn:
        attn_ref = refs.pop(0)
        wo_ref = refs.pop(0)
    g_ref, wup_ref, wdn_ref = refs[0:3]
    gf_ref = refs[3] if final_norm else None
    o_ref = refs[-1]

    x = x_ref[...]
    if has_attn:
        x = x + _dot(attn_ref[...], wo_ref[...])
    h = _rms(x, g_ref[...]).astype(bf16)
    a = jnp.maximum(_dot(h, wup_ref[...]), 0.0)
    y = x + _dot((a * a).astype(bf16), wdn_ref[...])
    if final_norm:
        y = _rms(y, gf_ref[...])
    o_ref[...] = y


def _mlp(x, g, w_up, w_down, attn=None, w_o=None, g_final=None):
    n = x.shape[0]
    tm = min(ROW_TILE, n)
    row = pl.BlockSpec((tm, D_MODEL), lambda i: (i, 0))
    in_specs, args = [row], [x]
    if attn is not None:
        in_specs += [row, _resident((D_MODEL, D_MODEL))]
        args += [attn, w_o]
    in_specs += [_resident((1, D_MODEL)), _resident((D_MODEL, D_FF)), _resident((D_FF, D_MODEL))]
    args += [g, w_up, w_down]
    if g_final is not None:
        in_specs.append(_resident((1, D_MODEL)))
        args.append(g_final)
    return pl.pallas_call(
        functools.partial(_mlp_kernel, has_attn=attn is not None, final_norm=g_final is not None),
        grid=(n // tm,),
        in_specs=in_specs,
        out_specs=row,
        out_shape=jax.ShapeDtypeStruct((n, D_MODEL), f32),
        compiler_params=_params(1),
        name="mlp",
    )(*args)


def _kv_kernel(*refs, tm, pack):
    if pack:
        (x_ref, g_ref, wk_ref, wv_ref, wf_ref, bf_ref, pk_ref, ckc_ref,
         k_ref, v_ref, lf_ref, c_ref, kp_ref, vt_ref, carry) = refs
    else:
        x_ref, g_ref, wk_ref, wv_ref, wf_ref, bf_ref, k_ref, v_ref, lf_ref = refs

    hk = _rms(x_ref[...], g_ref[...]).astype(bf16)
    k = _dot(hk, wk_ref[...])
    v = _dot(hk, wv_ref[...])
    z = _dot(hk, wf_ref[...]) + bf_ref[...]
    lane = lax.broadcasted_iota(jnp.int32, (tm, LANES), 1)
    logf = jnp.where(lane < N_HEADS, _log_sigmoid(z), 0.0)
    k_ref[...] = k
    v_ref[...] = v
    lf_ref[...] = logf[:, 0:N_HEADS]
    if not pack:
        return

    @pl.when(pl.program_id(0) == 0)
    def _():
        carry[...] = jnp.zeros_like(carry)

    c = _prefix_sum(_tri(tm), logf) + carry[0:1, :]
    carry[0:1, :] = c[tm - 1:tm, :]
    c2 = c * LOG2E
    c_ref[...] = c2
    hi, mid, lo = _split3(c2)
    extras = _dot(hi, pk_ref[0]) + _dot(mid, pk_ref[1]) + _dot(lo, pk_ref[2]) + ckc_ref[...]
    _pack_heads(k, lambda h: extras[:, h * LANES:(h + 1) * LANES], kp_ref)
    ones = jnp.where(lane == HEAD_DIM, 1.0, 0.0)
    for h in range(N_HEADS):
        pair = v[:, (h // 2) * LANES:(h // 2 + 1) * LANES]
        base = pair if h % 2 == 0 else pltpu.roll(pair, HEAD_DIM, axis=1)
        vt = jnp.transpose(jnp.where(lane < HEAD_DIM, base, ones))
        vt_ref[h, 0] = vt[0:VT_ROWS, :].astype(bf16)


def _kv(x, g, w_k, w_v, w_f, b_f, pk=None, ck_const=None):
    n = x.shape[0]
    tm = min(ROW_TILE, n)
    pack = pk is not None
    row = pl.BlockSpec((tm, D_MODEL), lambda i: (i, 0))
    heads = pl.BlockSpec((N_HEADS, tm, LANES), lambda i: (0, i, 0))
    in_specs = [row, _resident((1, D_MODEL)), _resident((D_MODEL, D_MODEL)), _resident((D_MODEL, D_MODEL)),
                _resident((D_MODEL, LANES)), _resident((1, LANES))]
    args = [x, g, w_k, w_v, w_f, b_f]
    out_specs = [row, row, pl.BlockSpec((tm, N_HEADS), lambda i: (i, 0))]
    out_shape = [jax.ShapeDtypeStruct((n, D_MODEL), f32), jax.ShapeDtypeStruct((n, D_MODEL), f32),
                 jax.ShapeDtypeStruct((n, N_HEADS), f32)]
    scratch = []
    if pack:
        in_specs += [_resident((3, LANES, N_HEADS * LANES)), _resident((1, N_HEADS * LANES))]
        args += [pk, ck_const]
        out_specs += [pl.BlockSpec((tm, LANES), lambda i: (i, 0)), heads,
                      pl.BlockSpec((N_HEADS, 1, VT_ROWS, tm), lambda i: (0, i, 0, 0))]
        out_shape += [jax.ShapeDtypeStruct((n, LANES), f32),
                      jax.ShapeDtypeStruct((N_HEADS, n, LANES), bf16),
                      jax.ShapeDtypeStruct((N_HEADS, n // tm, VT_ROWS, tm), bf16)]
        scratch = [pltpu.VMEM((8, LANES), f32)]
    return pl.pallas_call(
        functools.partial(_kv_kernel, tm=tm, pack=pack),
        grid=(n // tm,),
        in_specs=in_specs,
        out_specs=out_specs,
        out_shape=out_shape,
        scratch_shapes=scratch,
        compiler_params=_params(1),
        name="kv",
    )(*args)


def _qproj_kernel(*refs, pack):
    if pack:
        x_ref, g_ref, wq_ref, c_ref, pq_ref, cqc_ref, o_ref = refs
    else:
        x_ref, g_ref, wq_ref, o_ref = refs
    h = _rms(x_ref[...], g_ref[...]).astype(bf16)
    q = _dot(h, wq_ref[...]) * Q_SCALE
    if not pack:
        o_ref[...] = q
        return
    hi, mid, lo = _split3(c_ref[...])
    extras = _dot(hi, pq_ref[0]) + _dot(mid, pq_ref[1]) + _dot(lo, pq_ref[2]) + cqc_ref[...]
    _pack_heads(q, lambda h: extras, o_ref)


def _qproj(x, g, w_q, c2=None, pq=None, cq_const=None):
    n = x.shape[0]
    tm = min(ROW_TILE, n)
    pack = c2 is not None
    row = pl.BlockSpec((tm, D_MODEL), lambda i: (i, 0))
    in_specs = [row, _resident((1, D_MODEL)), _resident((D_MODEL, D_MODEL))]
    args = [x, g, w_q]
    if pack:
        in_specs += [pl.BlockSpec((tm, LANES), lambda i: (i, 0)), _resident((3, LANES, LANES)),
                     _resident((1, LANES))]
        args += [c2, pq, cq_const]
        out_spec = pl.BlockSpec((N_HEADS, tm, LANES), lambda i: (0, i, 0))
        out_shape = jax.ShapeDtypeStruct((N_HEADS, n, LANES), bf16)
    else:
        out_spec = row
        out_shape = jax.ShapeDtypeStruct((n, D_MODEL), f32)
    return pl.pallas_call(
        functools.partial(_qproj_kernel, pack=pack),
        grid=(n // tm,),
        in_specs=in_specs,
        out_specs=out_spec,
        out_shape=out_shape,
        compiler_params=_params(1),
        name="qproj",
    )(*args)


def _flash_kernel(q_ref, k_ref, vt_ref, o_ref, m_scr, acc_scr, s_even, s_odd, *, t):
    tq = 2 * t
    i = pl.program_id(1)
    for hh in range(2):
        m_scr[hh] = jnp.full((1, tq), NEG_BIG, f32)
        acc_scr[hh] = jnp.zeros((VT_ROWS, tq), f32)

    def logits(j, hh, c0):
        start = pl.multiple_of(j * t, t)
        q = q_ref[hh, c0:c0 + QUERY_CHUNK, :]
        return _dot_nt(k_ref[hh, pl.ds(start, t), :], q)

    def step(j, s_cur, s_next, diag_offset=None):
        for hh in range(2):
            for c0 in range(0, tq, QUERY_CHUNK):
                cols = slice(c0, c0 + QUERY_CHUNK)
                s = s_cur[hh, :, cols]
                if s_next is not None:
                    s_next[hh, :, cols] = logits(j + 1, hh, c0)
                if diag_offset is not None:
                    key = lax.broadcasted_iota(jnp.int32, (t, QUERY_CHUNK), 0) + (diag_offset - c0)
                    qry = lax.broadcasted_iota(jnp.int32, (t, QUERY_CHUNK), 1)
                    s = jnp.where(key <= qry, s, NEG_BIG)
                m_prev = m_scr[hh, :, cols]
                m_new = jnp.maximum(m_prev, jnp.max(s, axis=0, keepdims=True))
                p = jnp.exp2(s - m_new)
                pv = _dot(vt_ref[hh, j], p.astype(bf16))
                acc_scr[hh, :, cols] = jnp.exp2(m_prev - m_new) * acc_scr[hh, :, cols] + pv
                m_scr[hh, :, cols] = m_new

    for hh in range(2):
        for c0 in range(0, tq, QUERY_CHUNK):
            s_even[hh, :, c0:c0 + QUERY_CHUNK] = logits(0, hh, c0)

    def pair(jj, carry):
        step(2 * jj, s_even, s_odd)
        step(2 * jj + 1, s_odd, s_even)
        return carry

    lax.fori_loop(0, i, pair, 0)
    step(2 * i, s_even, s_odd, diag_offset=0)
    step(2 * i + 1, s_odd, None, diag_offset=t)

    outs = []
    for hh in range(2):
        acc = acc_scr[hh]
        outs.append(acc[0:HEAD_DIM, :] / acc[HEAD_DIM:HEAD_DIM + 1, :])
    o_ref[...] = jnp.transpose(jnp.concatenate(outs, axis=0)).astype(bf16)


def _flash(qp, kp, vt):
    n = qp.shape[1]
    t = ATT_TILE
    tq = 2 * t
    assert vt.shape == (N_HEADS, n // t, VT_ROWS, t) and n % tq == 0
    return pl.pallas_call(
        functools.partial(_flash_kernel, t=t),
        grid=(N_HEADS // 2, n // tq),
        in_specs=[pl.BlockSpec((2, tq, LANES), lambda p, i: (p, i, 0)),
                  pl.BlockSpec((2, n, LANES), lambda p, i: (p, 0, 0)),
                  pl.BlockSpec((2, n // t, VT_ROWS, t), lambda p, i: (p, 0, 0, 0))],
        out_specs=pl.BlockSpec((tq, LANES), lambda p, i: (i, p)),
        out_shape=jax.ShapeDtypeStruct((n, D_MODEL), bf16),
        scratch_shapes=[pltpu.VMEM((2, 1, tq), f32), pltpu.VMEM((2, VT_ROWS, tq), f32),
                        pltpu.VMEM((2, t, tq), f32), pltpu.VMEM((2, t, tq), f32)],
        compiler_params=_params(2),
        name="flash",
    )(qp, kp, vt)


def _cum_kernel(past_ref, new_ref, o_ref, *, past_len, new_len, chunk):
    t = _tri(chunk)
    carry = jnp.zeros((1, N_HEADS), f32)
    for s in range(0, past_len, chunk):
        c = _prefix_sum(t, past_ref[0, s:s + chunk, :]) + carry
        o_ref[0, s:s + chunk, :] = c * LOG2E
        carry = c[chunk - 1:chunk, :]
    c = _prefix_sum(_tri(new_len), new_ref[0]) + carry
    o_ref[0, past_len:past_len + new_len, :] = c * LOG2E


def _cum(past_logf, new_logf):
    b, past_len, _ = past_logf.shape
    new_len = new_logf.shape[1]
    total = past_len + new_len
    return pl.pallas_call(
        functools.partial(_cum_kernel, past_len=past_len, new_len=new_len, chunk=512),
        grid=(b,),
        in_specs=[pl.BlockSpec((1, past_len, N_HEADS), lambda i: (i, 0, 0)),
                  pl.BlockSpec((1, new_len, N_HEADS), lambda i: (i, 0, 0))],
        out_specs=pl.BlockSpec((1, total, N_HEADS), lambda i: (i, 0, 0)),
        out_shape=jax.ShapeDtypeStruct((b, total, N_HEADS), f32),
        compiler_params=_params(1),
        name="cum",
    )(past_logf, new_logf)


def _expand_heads(e, x):
    hi, mid, lo = _split3(x)
    return _dot_nt(e, hi) + _dot_nt(e, mid) + _dot_nt(e, lo)


def _decode_attn_kernel(q_ref, kc_ref, vc_ref, kn_ref, vn_ref, c_ref, e_ref, o_ref, *, past_len, new_len):
    rows = N_HEADS * new_len
    q = q_ref[0]
    row_head = lax.broadcasted_iota(jnp.int32, (rows, D_MODEL), 0) // new_len
    col_head = lax.broadcasted_iota(jnp.int32, (rows, D_MODEL), 1) // HEAD_DIM
    qbd = jnp.where(row_head == col_head, jnp.tile(q, (N_HEADS, 1)), 0.0).astype(bf16)
    e = e_ref[...]
    c2 = c_ref[0]
    ck_new = _expand_heads(e, c2[past_len:past_len + new_len, :])
    t_of_row = lax.broadcasted_iota(jnp.int32, (rows, new_len), 0) % new_len
    s_idx = lax.broadcasted_iota(jnp.int32, (rows, new_len), 1)
    cq = jnp.sum(jnp.where(s_idx == t_of_row, ck_new, 0.0), axis=1, keepdims=True)

    lg = _dot_nt(qbd, kc_ref[0]) + cq - _expand_heads(e, c2[0:past_len, :])
    lg_new = jnp.where(s_idx <= t_of_row, _dot_nt(qbd, kn_ref[0].astype(bf16)) + cq - ck_new, NEG_BIG)
    m = jnp.maximum(jnp.max(lg, axis=1, keepdims=True), jnp.max(lg_new, axis=1, keepdims=True))
    p = jnp.exp2(lg - m)
    p_new = jnp.exp2(lg_new - m)
    denom = jnp.sum(p, axis=1, keepdims=True) + jnp.sum(p_new, axis=1, keepdims=True)
    o = (_dot(p.astype(bf16), vc_ref[0])
         + _dot(p_new.astype(bf16), vn_ref[0].astype(bf16))) / denom
    lane = lax.broadcasted_iota(jnp.int32, (new_len, LANES), 1)
    for pp in range(N_HEADS // 2):
        even = o[(2 * pp) * new_len:(2 * pp + 1) * new_len, pp * LANES:(pp + 1) * LANES]
        odd = o[(2 * pp + 1) * new_len:(2 * pp + 2) * new_len, pp * LANES:(pp + 1) * LANES]
        o_ref[0, :, pp * LANES:(pp + 1) * LANES] = jnp.where(lane < HEAD_DIM, even, odd).astype(bf16)


def _decode_attn(q, cache_k, cache_v, k_new, v_new, c2, expand):
    b, past_len, _ = cache_k.shape
    new_len = q.shape[1]
    cache = pl.BlockSpec((1, past_len, D_MODEL), lambda i: (i, 0, 0))
    new = pl.BlockSpec((1, new_len, D_MODEL), lambda i: (i, 0, 0))
    return pl.pallas_call(
        functools.partial(_decode_attn_kernel, past_len=past_len, new_len=new_len),
        grid=(b,),
        in_specs=[new, cache, cache, new, new,
                  pl.BlockSpec((1, past_len + new_len, N_HEADS), lambda i: (i, 0, 0)),
                  _resident((N_HEADS * new_len, N_HEADS))],
        out_specs=new,
        out_shape=jax.ShapeDtypeStruct((b, new_len, D_MODEL), bf16),
        compiler_params=_params(1),
        name="decode_attn",
    )(q, cache_k, cache_v, k_new, v_new, c2, expand)


def kernel(x_prompt, x_sample, cache_k, cache_v, cache_logf, state_conv, norm_mix, norm_mlp, conv_in, conv_w,
           conv_out, norm_kv, w_k, w_v, w_f, b_f, w_q, w_o, mlp_up, mlp_down, norm_final):
    assert x_prompt.shape[0] == 1 and x_prompt.shape[2] == D_MODEL
    seq = x_prompt.shape[1]
    dec_b, dec_s, _ = x_sample.shape
    past_len = cache_k.shape[1]
    assert seq % ROW_TILE == 0 and ROW_TILE == ATT_TILE and (dec_b * dec_s) % 8 == 0

    pk, ck_const, pq, cq_const, expand = _placement_constants()
    row = lambda a: a.reshape(1, -1)
    conv_in_b, conv_out_b = conv_in.astype(bf16), conv_out.astype(bf16)
    w_k_b, w_v_b, w_q_b, w_o_b = w_k.astype(bf16), w_v.astype(bf16), w_q.astype(bf16), w_o.astype(bf16)
    up_b, down_b = mlp_up.astype(bf16), mlp_down.astype(bf16)
    w_f_b = jnp.pad(w_f, ((0, 0), (0, LANES - N_HEADS))).astype(bf16)
    b_f_p = jnp.pad(b_f, (0, LANES - N_HEADS)).reshape(1, LANES)

    def mlp(x, l, **kw):
        return _mlp(x, row(norm_mlp[l]), up_b[l], down_b[l], **kw)

    x = x_prompt.reshape(seq, D_MODEL)
    tails = []
    for l in range(N_A):
        x, tail = _mixer(x, row(norm_mix[l]), conv_in_b[l], conv_w[l], conv_out_b[l])
        tails.append(tail[6:8])
        x = mlp(x, l)
    k_p, v_p, logf_p, c2_p, kp, vt = _kv(x, row(norm_kv), w_k_b, w_v_b, w_f_b, b_f_p, pk, ck_const)
    for l in range(N_A, DEPTH):
        j = l - N_A
        qp = _qproj(x, row(norm_mix[l]), w_q_b[j], c2_p, pq, cq_const)
        attn = _flash(qp, kp, vt)
        x = mlp(x, l, attn=attn, w_o=w_o_b[j], g_final=row(norm_final) if l == DEPTH - 1 else None)
    y_prompt = x.reshape(1, seq, D_MODEL)
    conv_prompt = jnp.stack(tails).reshape(N_A, 1, 2, D_MODEL)

    n_s = dec_b * dec_s
    x = x_sample.reshape(n_s, D_MODEL)
    new_states = []
    for l in range(N_A):
        st = state_conv[l]
        pad = jnp.zeros((dec_b, dec_s - 2, D_MODEL), f32)
        s2 = jnp.concatenate([st, pad], axis=1).reshape(n_s, D_MODEL)
        s1 = jnp.concatenate([st[:, 1:2], pad, pad[:, 0:1]], axis=1).reshape(n_s, D_MODEL)
        x, u = _mixer(x, row(norm_mix[l]), conv_in_b[l], conv_w[l], conv_out_b[l],
                      state_rows=(s1, s2), seq_len=dec_s)
        new_states.append(u.reshape(dec_b, dec_s, D_MODEL)[:, dec_s - 2:])
        x = mlp(x, l)
    k_s, v_s, logf_s = _kv(x, row(norm_kv), w_k_b, w_v_b, w_f_b, b_f_p)
    c2_s = _cum(cache_logf, logf_s.reshape(dec_b, dec_s, N_HEADS))
    k_s3, v_s3 = k_s.reshape(dec_b, dec_s, D_MODEL), v_s.reshape(dec_b, dec_s, D_MODEL)
    cache_k3 = cache_k.astype(bf16).reshape(dec_b, past_len, D_MODEL)
    cache_v3 = cache_v.astype(bf16).reshape(dec_b, past_len, D_MODEL)
    for l in range(N_A, DEPTH):
        j = l - N_A
        q = _qproj(x, row(norm_mix[l]), w_q_b[j])
        attn = _decode_attn(q.reshape(dec_b, dec_s, D_MODEL), cache_k3, cache_v3, k_s3, v_s3, c2_s, expand)
        x = mlp(x, l, attn=attn.reshape(n_s, D_MODEL), w_o=w_o_b[j],
                g_final=row(norm_final) if l == DEPTH - 1 else None)
    y_sample = x.reshape(dec_b, dec_s, D_MODEL)

    return (y_prompt, y_sample,
            k_p.reshape(1, seq, N_HEADS, HEAD_DIM), v_p.reshape(1, seq, N_HEADS, HEAD_DIM),
            logf_p.reshape(1, seq, N_HEADS), conv_prompt,
            k_s.reshape(dec_b, dec_s, N_HEADS, HEAD_DIM), v_s.reshape(dec_b, dec_s, N_HEADS, HEAD_DIM),
            logf_s.reshape(dec_b, dec_s, N_HEADS), jnp.stack(new_states))
```

```python
import functools
import math

import numpy as np
import jax
import jax.numpy as jnp
from jax import lax
from jax.experimental import pallas as pl
from jax.experimental.pallas import tpu as pltpu

D_MODEL = 1024
N_HEADS = 16
HEAD_DIM = 64
D_FF = 4 * D_MODEL
N_A = 2
DEPTH = 4
RMS_EPS = 1e-6
LOG2E = math.log2(math.e)
Q_SCALE = HEAD_DIM ** -0.5 * LOG2E
NEG_BIG = -1e30

LANES = 128
ROW_TILE = 512
ATT_TILE = 512
QUERY_CHUNK = 512
VMEM_LIMIT = 56 * 1024 * 1024

CQ_LANE0 = 64
CK_LANE0 = 112
VT_ROWS = 80

bf16 = jnp.bfloat16
f32 = jnp.float32


def _dot(a, b):
    return jnp.dot(a, b, preferred_element_type=f32)


def _dot_nt(a, b):
    return lax.dot_general(a, b, (((1,), (1,)), ((), ())), preferred_element_type=f32)


def _rms(x, g):
    ms = jnp.mean(x * x, axis=-1, keepdims=True)
    return x * lax.rsqrt(ms + RMS_EPS) * g


def _split3(x):
    hi = x.astype(bf16)
    r1 = x - hi.astype(f32)
    mid = r1.astype(bf16)
    lo = (r1 - mid.astype(f32)).astype(bf16)
    return hi, mid, lo


def _tri(n):
    r = lax.broadcasted_iota(jnp.int32, (n, n), 0)
    c = lax.broadcasted_iota(jnp.int32, (n, n), 1)
    return jnp.where(c <= r, 1.0, 0.0).astype(bf16)


def _prefix_sum(t, x):
    hi, mid, lo = _split3(x)
    return _dot(t, hi) + _dot(t, mid) + _dot(t, lo)


def _log_sigmoid(z):
    return jnp.minimum(z, 0.0) - jnp.log1p(jnp.exp(-jnp.abs(z)))


def _resident(shape):
    nd = len(shape)
    return pl.BlockSpec(shape, lambda *_: (0,) * nd, pipeline_mode=pl.Buffered(1))


def _layer(shape, layer):
    nd = len(shape)
    return pl.BlockSpec((None,) + tuple(shape), lambda *_: (layer,) + (0,) * nd, pipeline_mode=pl.Buffered(1))


def _params(n_axes):
    return pltpu.CompilerParams(dimension_semantics=("arbitrary",) * n_axes,
                                vmem_limit_bytes=VMEM_LIMIT)


def _placement_constants():
    pk = np.zeros((3, LANES, N_HEADS * LANES), np.float32)
    ck_const = np.zeros((1, N_HEADS * LANES), np.float32)
    pq = np.zeros((3, LANES, LANES), np.float32)
    cq_const = np.zeros((1, LANES), np.float32)
    for j in range(3):
        cq_const[0, CK_LANE0 + j] = 1.0
        for h in range(N_HEADS):
            pk[j, h, h * LANES + CK_LANE0 + j] = -1.0
            ck_const[0, h * LANES + CQ_LANE0 + 16 * j + h] = 1.0
            pq[j, h, CQ_LANE0 + 16 * j + h] = 1.0
    expand = np.zeros((N_HEADS * 16, N_HEADS), np.float32)
    for h in range(N_HEADS):
        expand[h * 16:(h + 1) * 16, h] = 1.0
    return (jnp.asarray(pk, bf16), jnp.asarray(ck_const), jnp.asarray(pq, bf16),
            jnp.asarray(cq_const), jnp.asarray(expand, bf16))


def _pack_heads(x, extras_of_head, out_ref):
    lane = lax.broadcasted_iota(jnp.int32, (x.shape[0], LANES), 1)
    for h in range(N_HEADS):
        pair = x[:, (h // 2) * LANES:(h // 2 + 1) * LANES]
        base = pair if h % 2 == 0 else pltpu.roll(pair, HEAD_DIM, axis=1)
        out_ref[h] = jnp.where(lane < HEAD_DIM, base, extras_of_head(h)).astype(bf16)


def _mixer_kernel(*refs, tm, carry_rows, seq_len):
    if carry_rows:
        x_ref, g_ref, win_ref, cw_ref, wout_ref, o_ref, u_ref, u_scr = refs
    else:
        x_ref, g_ref, win_ref, cw_ref, wout_ref, s1_ref, s2_ref, o_ref, u_ref, u_scr = refs

    @pl.when(pl.program_id(0) == 0)
    def _():
        u_scr[0:8, :] = jnp.zeros((8, D_MODEL), f32)

    x = x_ref[...]
    h = _rms(x, g_ref[...]).astype(bf16)
    bcx = _dot(h, win_ref[...])
    gb = bcx[:, 0:D_MODEL]
    u = bcx[:, D_MODEL:2 * D_MODEL] * bcx[:, 2 * D_MODEL:3 * D_MODEL]
    u_scr[8:8 + tm, :] = u
    u1 = u_scr[7:7 + tm, :]
    u2 = u_scr[6:6 + tm, :]
    if carry_rows:
        tail = u_scr[tm:tm + 8, :]
        u_scr[0:8, :] = tail
        u_ref[...] = tail
    else:
        pos = lax.broadcasted_iota(jnp.int32, (tm, 1), 0) % seq_len
        u1 = jnp.where(pos < 1, s1_ref[...], u1)
        u2 = jnp.where(pos < 2, s2_ref[...], u2)
        u_ref[...] = u
    cw = cw_ref[...]
    conv = cw[0:1, :] * u2 + cw[1:2, :] * u1 + cw[2:3, :] * u
    o_ref[...] = x + _dot((gb * conv).astype(bf16), wout_ref[...])


def _mixer(x, g, w_in, cw, w_out, layer, state_rows=None, seq_len=None):
    n = x.shape[0]
    tm = min(ROW_TILE, n)
    carry = state_rows is None
    row = pl.BlockSpec((tm, D_MODEL), lambda i: (i, 0))
    in_specs = [row, _resident((1, D_MODEL)), _layer((D_MODEL, 3 * D_MODEL), layer),
                _layer((3, D_MODEL), layer), _layer((D_MODEL, D_MODEL), layer)]
    args = [x, g, w_in, cw, w_out]
    if carry:
        u_spec = pl.BlockSpec((8, D_MODEL), lambda i: (0, 0))
        u_shape = jax.ShapeDtypeStruct((8, D_MODEL), f32)
    else:
        assert n == tm
        in_specs += [row, row]
        args += list(state_rows)
        u_spec = row
        u_shape = jax.ShapeDtypeStruct((n, D_MODEL), f32)
    return pl.pallas_call(
        functools.partial(_mixer_kernel, tm=tm, carry_rows=carry, seq_len=seq_len),
        grid=(n // tm,),
        in_specs=in_specs,
        out_specs=[row, u_spec],
        out_shape=[jax.ShapeDtypeStruct((n, D_MODEL), f32), u_shape],
        scratch_shapes=[pltpu.VMEM((tm + 8, D_MODEL), f32)],
        compiler_params=_params(1),
        name="mixer",
    )(*args)


def _mlp_kernel(*refs, has_attn, final_norm):
    refs = list(refs)
    x_ref = refs.pop(0)
    if has_attn:
        attn_ref = refs.pop(0)
        wo_ref = refs.pop(0)
    g_ref, wup_ref, wdn_ref = refs[0:3]
    gf_ref = refs[3] if final_norm else None
    o_ref = refs[-1]

    x = x_ref[...]
    if has_attn:
        x = x + _dot(attn_ref[...], wo_ref[...])
    h = _rms(x, g_ref[...]).astype(bf16)
    a = jnp.maximum(_dot(h, wup_ref[...]), 0.0)
    y = x + _dot((a * a).astype(bf16), wdn_ref[...])
    if final_norm:
        y = _rms(y, gf_ref[...])
    o_ref[...] = y


def _mlp(x, g, w_up, w_down, layer, attn=None, w_o=None, attn_layer=None, g_final=None):
    n = x.shape[0]
    tm = min(ROW_TILE, n)
    row = pl.BlockSpec((tm, D_MODEL), lambda i: (i, 0))
    in_specs, args = [row], [x]
    if attn is not None:
        in_specs += [row, _layer((D_MODEL, D_MODEL), attn_layer)]
        args += [attn, w_o]
    in_specs += [_resident((1, D_MODEL)), _layer((D_MODEL, D_FF), layer), _layer((D_FF, D_MODEL), layer)]
    args += [g, w_up, w_down]
    if g_final is not None:
        in_specs.append(_resident((1, D_MODEL)))
        args.append(g_final)
    return pl.pallas_call(
        functools.partial(_mlp_kernel, has_attn=attn is not None, final_norm=g_final is not None),
        grid=(n // tm,),
        in_specs=in_specs,
        out_specs=row,
        out_shape=jax.ShapeDtypeStruct((n, D_MODEL), f32),
        compiler_params=_params(1),
        name="mlp",
    )(*args)


def _kv_kernel(*refs, tm, pack):
    if pack:
        (x_ref, g_ref, wk_ref, wv_ref, wf_ref, bf_ref, pk_ref, ckc_ref,
         k_ref, v_ref, lf_ref, c_ref, kp_ref, vt_ref, carry) = refs
    else:
        x_ref, g_ref, wk_ref, wv_ref, wf_ref, bf_ref, k_ref, v_ref, lf_ref = refs

    hk = _rms(x_ref[...], g_ref[...]).astype(bf16)
    k = _dot(hk, wk_ref[...])
    v = _dot(hk, wv_ref[...])
    z = _dot(hk, wf_ref[...]) + bf_ref[...]
    lane = lax.broadcasted_iota(jnp.int32, (tm, LANES), 1)
    logf = jnp.where(lane < N_HEADS, _log_sigmoid(z), 0.0)
    k_ref[...] = k
    v_ref[...] = v
    lf_ref[...] = logf[:, 0:N_HEADS]
    if not pack:
        return

    @pl.when(pl.program_id(0) == 0)
    def _():
        carry[...] = jnp.zeros_like(carry)

    c = _prefix_sum(_tri(tm), logf) + carry[0:1, :]
    carry[0:1, :] = c[tm - 1:tm, :]
    c2 = c * LOG2E
    c_ref[...] = c2
    hi, mid, lo = _split3(c2)
    extras = _dot(hi, pk_ref[0]) + _dot(mid, pk_ref[1]) + _dot(lo, pk_ref[2]) + ckc_ref[...]
    _pack_heads(k, lambda h: extras[:, h * LANES:(h + 1) * LANES], kp_ref)
    pad_row = lax.broadcasted_iota(jnp.int32, (VT_ROWS - HEAD_DIM, tm), 0)
    ones_row = jnp.where(pad_row == 0, 1.0, 0.0)
    for p in range(N_HEADS // 2):
        vt = jnp.transpose(v[:, p * LANES:(p + 1) * LANES])
        for hh in range(2):
            rows = vt[hh * HEAD_DIM:(hh + 1) * HEAD_DIM, :]
            vt_ref[2 * p + hh, 0] = jnp.concatenate([rows, ones_row], axis=0).astype(bf16)


def _kv(x, g, w_k, w_v, w_f, b_f, pk=None, ck_const=None):
    n = x.shape[0]
    tm = min(ROW_TILE, n)
    pack = pk is not None
    row = pl.BlockSpec((tm, D_MODEL), lambda i: (i, 0))
    heads = pl.BlockSpec((N_HEADS, tm, LANES), lambda i: (0, i, 0))
    in_specs = [row, _resident((1, D_MODEL)), _resident((D_MODEL, D_MODEL)), _resident((D_MODEL, D_MODEL)),
                _resident((D_MODEL, LANES)), _resident((1, LANES))]
    args = [x, g, w_k, w_v, w_f, b_f]
    out_specs = [row, row, pl.BlockSpec((tm, N_HEADS), lambda i: (i, 0))]
    out_shape = [jax.ShapeDtypeStruct((n, D_MODEL), f32), jax.ShapeDtypeStruct((n, D_MODEL), f32),
                 jax.ShapeDtypeStruct((n, N_HEADS), f32)]
    scratch = []
    if pack:
        in_specs += [_resident((3, LANES, N_HEADS * LANES)), _resident((1, N_HEADS * LANES))]
        args += [pk, ck_const]
        out_specs += [pl.BlockSpec((tm, LANES), lambda i: (i, 0)), heads,
                      pl.BlockSpec((N_HEADS, 1, VT_ROWS, tm), lambda i: (0, i, 0, 0))]
        out_shape += [jax.ShapeDtypeStruct((n, LANES), f32),
                      jax.ShapeDtypeStruct((N_HEADS, n, LANES), bf16),
                      jax.ShapeDtypeStruct((N_HEADS, n // tm, VT_ROWS, tm), bf16)]
        scratch = [pltpu.VMEM((8, LANES), f32)]
    return pl.pallas_call(
        functools.partial(_kv_kernel, tm=tm, pack=pack),
        grid=(n // tm,),
        in_specs=in_specs,
        out_specs=out_specs,
        out_shape=out_shape,
        scratch_shapes=scratch,
        compiler_params=_params(1),
        name="kv",
    )(*args)


def _qproj_kernel(*refs, pack):
    if pack:
        x_ref, g_ref, wq_ref, c_ref, pq_ref, cqc_ref, o_ref = refs
    else:
        x_ref, g_ref, wq_ref, o_ref = refs
    h = _rms(x_ref[...], g_ref[...]).astype(bf16)
    q = _dot(h, wq_ref[...]) * Q_SCALE
    if not pack:
        o_ref[...] = q
        return
    hi, mid, lo = _split3(c_ref[...])
    extras = _dot(hi, pq_ref[0]) + _dot(mid, pq_ref[1]) + _dot(lo, pq_ref[2]) + cqc_ref[...]
    _pack_heads(q, lambda h: extras, o_ref)


def _qproj(x, g, w_q, layer, c2=None, pq=None, cq_const=None):
    n = x.shape[0]
    tm = min(ROW_TILE, n)
    pack = c2 is not None
    row = pl.BlockSpec((tm, D_MODEL), lambda i: (i, 0))
    in_specs = [row, _resident((1, D_MODEL)), _layer((D_MODEL, D_MODEL), layer)]
    args = [x, g, w_q]
    if pack:
        in_specs += [pl.BlockSpec((tm, LANES), lambda i: (i, 0)), _resident((3, LANES, LANES)),
                     _resident((1, LANES))]
        args += [c2, pq, cq_const]
        out_spec = pl.BlockSpec((N_HEADS, tm, LANES), lambda i: (0, i, 0))
        out_shape = jax.ShapeDtypeStruct((N_HEADS, n, LANES), bf16)
    else:
        out_spec = row
        out_shape = jax.ShapeDtypeStruct((n, D_MODEL), f32)
    return pl.pallas_call(
        functools.partial(_qproj_kernel, pack=pack),
        grid=(n // tm,),
        in_specs=in_specs,
        out_specs=out_spec,
        out_shape=out_shape,
        compiler_params=_params(1),
        name="qproj",
    )(*args)


def _flash_kernel(q_ref, k_ref, vt_ref, o_ref, m_scr, acc_scr, s_even, s_odd, *, t):
    tq = 2 * t
    i = pl.program_id(1)
    for hh in range(2):
        m_scr[hh] = jnp.full((1, tq), NEG_BIG, f32)
        acc_scr[hh] = jnp.zeros((VT_ROWS, tq), f32)

    def logits(j, hh, c0):
        start = pl.multiple_of(j * t, t)
        q = q_ref[hh, c0:c0 + QUERY_CHUNK, :]
        return _dot_nt(k_ref[hh, pl.ds(start, t), :], q)

    def step(j, s_cur, s_next, diag_offset=None):
        for hh in range(2):
            for c0 in range(0, tq, QUERY_CHUNK):
                cols = slice(c0, c0 + QUERY_CHUNK)
                s = s_cur[hh, :, cols]
                if s_next is not None:
                    s_next[hh, :, cols] = logits(j + 1, hh, c0)
                if diag_offset is not None:
                    key = lax.broadcasted_iota(jnp.int32, (t, QUERY_CHUNK), 0) + (diag_offset - c0)
                    qry = lax.broadcasted_iota(jnp.int32, (t, QUERY_CHUNK), 1)
                    s = jnp.where(key <= qry, s, NEG_BIG)
                m_prev = m_scr[hh, :, cols]
                m_new = jnp.maximum(m_prev, jnp.max(s, axis=0, keepdims=True))
                p = jnp.exp2(s - m_new)
                pv = _dot(vt_ref[hh, j], p.astype(bf16))
                acc_scr[hh, :, cols] = jnp.exp2(m_prev - m_new) * acc_scr[hh, :, cols] + pv
                m_scr[hh, :, cols] = m_new

    for hh in range(2):
        for c0 in range(0, tq, QUERY_CHUNK):
            s_even[hh, :, c0:c0 + QUERY_CHUNK] = logits(0, hh, c0)

    def pair(jj, carry):
        step(2 * jj, s_even, s_odd)
        step(2 * jj + 1, s_odd, s_even)
        return carry

    lax.fori_loop(0, i, pair, 0)
    step(2 * i, s_even, s_odd, diag_offset=0)
    step(2 * i + 1, s_odd, None, diag_offset=t)

    outs = []
    for hh in range(2):
        acc = acc_scr[hh]
        outs.append(acc[0:HEAD_DIM, :] / acc[HEAD_DIM:HEAD_DIM + 1, :])
    o_ref[...] = jnp.transpose(jnp.concatenate(outs, axis=0)).astype(bf16)


def _flash(qp, kp, vt):
    n = qp.shape[1]
    t = ATT_TILE
    tq = 2 * t
    assert vt.shape == (N_HEADS, n // t, VT_ROWS, t) and n % tq == 0
    return pl.pallas_call(
        functools.partial(_flash_kernel, t=t),
        grid=(N_HEADS // 2, n // tq),
        in_specs=[pl.BlockSpec((2, tq, LANES), lambda p, i: (p, i, 0)),
                  pl.BlockSpec((2, n, LANES), lambda p, i: (p, 0, 0)),
                  pl.BlockSpec((2, n // t, VT_ROWS, t), lambda p, i: (p, 0, 0, 0))],
        out_specs=pl.BlockSpec((tq, LANES), lambda p, i: (i, p)),
        out_shape=jax.ShapeDtypeStruct((n, D_MODEL), bf16),
        scratch_shapes=[pltpu.VMEM((2, 1, tq), f32), pltpu.VMEM((2, VT_ROWS, tq), f32),
                        pltpu.VMEM((2, t, tq), f32), pltpu.VMEM((2, t, tq), f32)],
        compiler_params=_params(2),
        name="flash",
    )(qp, kp, vt)


def _cum_kernel(past_ref, new_ref, o_ref, *, past_len, new_len, chunk):
    t = _tri(chunk)
    carry = jnp.zeros((1, N_HEADS), f32)
    for s in range(0, past_len, chunk):
        c = _prefix_sum(t, past_ref[0, s:s + chunk, :]) + carry
        o_ref[0, s:s + chunk, :] = c * LOG2E
        carry = c[chunk - 1:chunk, :]
    c = _prefix_sum(_tri(new_len), new_ref[0]) + carry
    o_ref[0, past_len:past_len + new_len, :] = c * LOG2E


def _cum(past_logf, new_logf):
    b, past_len, _ = past_logf.shape
    new_len = new_logf.shape[1]
    total = past_len + new_len
    return pl.pallas_call(
        functools.partial(_cum_kernel, past_len=past_len, new_len=new_len, chunk=512),
        grid=(b,),
        in_specs=[pl.BlockSpec((1, past_len, N_HEADS), lambda i: (i, 0, 0)),
                  pl.BlockSpec((1, new_len, N_HEADS), lambda i: (i, 0, 0))],
        out_specs=pl.BlockSpec((1, total, N_HEADS), lambda i: (i, 0, 0)),
        out_shape=jax.ShapeDtypeStruct((b, total, N_HEADS), f32),
        compiler_params=_params(1),
        name="cum",
    )(past_logf, new_logf)


def _expand_heads(e, x):
    hi, mid, lo = _split3(x)
    return _dot_nt(e, hi) + _dot_nt(e, mid) + _dot_nt(e, lo)


def _decode_attn_kernel(q_ref, kc_ref, vc_ref, kn_ref, vn_ref, c_ref, e_ref, o_ref, *, past_len, new_len):
    rows = N_HEADS * new_len
    q = q_ref[0]
    row_head = lax.broadcasted_iota(jnp.int32, (rows, D_MODEL), 0) // new_len
    col_head = lax.broadcasted_iota(jnp.int32, (rows, D_MODEL), 1) // HEAD_DIM
    qbd = jnp.where(row_head == col_head, jnp.tile(q, (N_HEADS, 1)), 0.0).astype(bf16)
    e = e_ref[...]
    c2 = c_ref[0]
    ck_new = _expand_heads(e, c2[past_len:past_len + new_len, :])
    t_of_row = lax.broadcasted_iota(jnp.int32, (rows, new_len), 0) % new_len
    s_idx = lax.broadcasted_iota(jnp.int32, (rows, new_len), 1)
    cq = jnp.sum(jnp.where(s_idx == t_of_row, ck_new, 0.0), axis=1, keepdims=True)

    lg = _dot_nt(qbd, kc_ref[0]) + cq - _expand_heads(e, c2[0:past_len, :])
    lg_new = jnp.where(s_idx <= t_of_row, _dot_nt(qbd, kn_ref[0].astype(bf16)) + cq - ck_new, NEG_BIG)
    m = jnp.maximum(jnp.max(lg, axis=1, keepdims=True), jnp.max(lg_new, axis=1, keepdims=True))
    p = jnp.exp2(lg - m)
    p_new = jnp.exp2(lg_new - m)
    denom = jnp.sum(p, axis=1, keepdims=True) + jnp.sum(p_new, axis=1, keepdims=True)
    o = (_dot(p.astype(bf16), vc_ref[0])
         + _dot(p_new.astype(bf16), vn_ref[0].astype(bf16))) / denom
    lane = lax.broadcasted_iota(jnp.int32, (new_len, LANES), 1)
    for pp in range(N_HEADS // 2):
        even = o[(2 * pp) * new_len:(2 * pp + 1) * new_len, pp * LANES:(pp + 1) * LANES]
        odd = o[(2 * pp + 1) * new_len:(2 * pp + 2) * new_len, pp * LANES:(pp + 1) * LANES]
        o_ref[0, :, pp * LANES:(pp + 1) * LANES] = jnp.where(lane < HEAD_DIM, even, odd).astype(bf16)


def _decode_attn(q, cache_k, cache_v, k_new, v_new, c2, expand):
    b, past_len, _ = cache_k.shape
    new_len = q.shape[1]
    cache = pl.BlockSpec((1, past_len, D_MODEL), lambda i: (i, 0, 0))
    new = pl.BlockSpec((1, new_len, D_MODEL), lambda i: (i, 0, 0))
    return pl.pallas_call(
        functools.partial(_decode_attn_kernel, past_len=past_len, new_len=new_len),
        grid=(b,),
        in_specs=[new, cache, cache, new, new,
                  pl.BlockSpec((1, past_len + new_len, N_HEADS), lambda i: (i, 0, 0)),
                  _resident((N_HEADS * new_len, N_HEADS))],
        out_specs=new,
        out_shape=jax.ShapeDtypeStruct((b, new_len, D_MODEL), bf16),
        compiler_params=_params(1),
        name="decode_attn",
    )(q, cache_k, cache_v, k_new, v_new, c2, expand)


def kernel(x_prompt, x_sample, cache_k, cache_v, cache_logf, state_conv, norm_mix, norm_mlp, conv_in, conv_w,
           conv_out, norm_kv, w_k, w_v, w_f, b_f, w_q, w_o, mlp_up, mlp_down, norm_final):
    assert x_prompt.shape[0] == 1 and x_prompt.shape[2] == D_MODEL
    seq = x_prompt.shape[1]
    dec_b, dec_s, _ = x_sample.shape
    past_len = cache_k.shape[1]
    assert seq % ROW_TILE == 0 and ROW_TILE == ATT_TILE and (dec_b * dec_s) % 8 == 0

    pk, ck_const, pq, cq_const, expand = _placement_constants()
    row = lambda a: a.reshape(1, -1)
    conv_in_b, conv_out_b = conv_in.astype(bf16), conv_out.astype(bf16)
    w_k_b, w_v_b, w_q_b, w_o_b = w_k.astype(bf16), w_v.astype(bf16), w_q.astype(bf16), w_o.astype(bf16)
    up_b, down_b = mlp_up.astype(bf16), mlp_down.astype(bf16)
    w_f_b = jnp.pad(w_f, ((0, 0), (0, LANES - N_HEADS))).astype(bf16)
    b_f_p = jnp.pad(b_f, (0, LANES - N_HEADS)).reshape(1, LANES)

    def mlp(x, l, **kw):
        return _mlp(x, row(norm_mlp[l]), up_b, down_b, l, **kw)

    x = x_prompt.reshape(seq, D_MODEL)
    tails = []
    for l in range(N_A):
        x, tail = _mixer(x, row(norm_mix[l]), conv_in_b, conv_w, conv_out_b, l)
        tails.append(tail[6:8])
        x = mlp(x, l)
    k_p, v_p, logf_p, c2_p, kp, vt = _kv(x, row(norm_kv), w_k_b, w_v_b, w_f_b, b_f_p, pk, ck_const)
    for l in range(N_A, DEPTH):
        j = l - N_A
        qp = _qproj(x, row(norm_mix[l]), w_q_b, j, c2_p, pq, cq_const)
        attn = _flash(qp, kp, vt)
        x = mlp(x, l, attn=attn, w_o=w_o_b, attn_layer=j,
                g_final=row(norm_final) if l == DEPTH - 1 else None)
    y_prompt = x.reshape(1, seq, D_MODEL)
    conv_prompt = jnp.stack(tails).reshape(N_A, 1, 2, D_MODEL)

    n_s = dec_b * dec_s
    x = x_sample.reshape(n_s, D_MODEL)
    new_states = []
    for l in range(N_A):
        st = state_conv[l]
        pad = jnp.zeros((dec_b, dec_s - 2, D_MODEL), f32)
        s2 = jnp.concatenate([st, pad], axis=1).reshape(n_s, D_MODEL)
        s1 = jnp.concatenate([st[:, 1:2], pad, pad[:, 0:1]], axis=1).reshape(n_s, D_MODEL)
        x, u = _mixer(x, row(norm_mix[l]), conv_in_b, conv_w, conv_out_b, l,
                      state_rows=(s1, s2), seq_len=dec_s)
        new_states.append(u.reshape(dec_b, dec_s, D_MODEL)[:, dec_s - 2:])
        x = mlp(x, l)
    k_s, v_s, logf_s = _kv(x, row(norm_kv), w_k_b, w_v_b, w_f_b, b_f_p)
    c2_s = _cum(cache_logf, logf_s.reshape(dec_b, dec_s, N_HEADS))
    k_s3, v_s3 = k_s.reshape(dec_b, dec_s, D_MODEL), v_s.reshape(dec_b, dec_s, D_MODEL)
    cache_k3 = cache_k.astype(bf16).reshape(dec_b, past_len, D_MODEL)
    cache_v3 = cache_v.astype(bf16).reshape(dec_b, past_len, D_MODEL)
    for l in range(N_A, DEPTH):
        j = l - N_A
        q = _qproj(x, row(norm_mix[l]), w_q_b, j)
        attn = _decode_attn(q.reshape(dec_b, dec_s, D_MODEL), cache_k3, cache_v3, k_s3, v_s3, c2_s, expand)
        x = mlp(x, l, attn=attn.reshape(n_s, D_MODEL), w_o=w_o_b, attn_layer=j,
                g_final=row(norm_final) if l == DEPTH - 1 else None)
    y_sample = x.reshape(dec_b, dec_s, D_MODEL)

    return (y_prompt, y_sample,
            k_p.reshape(1, seq, N_HEADS, HEAD_DIM), v_p.reshape(1, seq, N_HEADS, HEAD_DIM),
            logf_p.reshape(1, seq, N_HEADS), conv_prompt,
            k_s.reshape(dec_b, dec_s, N_HEADS, HEAD_DIM), v_s.reshape(dec_b, dec_s, N_HEADS, HEAD_DIM),
            logf_s.reshape(dec_b, dec_s, N_HEADS), jnp.stack(new_states))
```

```python
import functools
import math

import numpy as np
import jax
import jax.numpy as jnp
from jax import lax
from jax.experimental import pallas as pl
from jax.experimental.pallas import tpu as pltpu

D_MODEL = 1024
N_HEADS = 16
HEAD_DIM = 64
D_FF = 4 * D_MODEL
N_A = 2
DEPTH = 4
RMS_EPS = 1e-6
LOG2E = math.log2(math.e)
Q_SCALE = HEAD_DIM ** -0.5 * LOG2E
NEG_BIG = -1e30

LANES = 128
ROW_TILE = 512
ATT_TILE = 512
QUERY_CHUNK = 256
SKIP_MARGIN = 170.0
NORM_SLACK = 1.01
VMEM_LIMIT = 56 * 1024 * 1024

CQ_LANE0 = 64
CK_LANE0 = 112
VT_ROWS = 80

bf16 = jnp.bfloat16
f32 = jnp.float32


def _dot(a, b):
    return jnp.dot(a, b, preferred_element_type=f32)


def _dot_nt(a, b):
    return lax.dot_general(a, b, (((1,), (1,)), ((), ())), preferred_element_type=f32)


def _rms(x, g):
    ms = jnp.mean(x * x, axis=-1, keepdims=True)
    return x * lax.rsqrt(ms + RMS_EPS) * g


def _split3(x):
    hi = x.astype(bf16)
    r1 = x - hi.astype(f32)
    mid = r1.astype(bf16)
    lo = (r1 - mid.astype(f32)).astype(bf16)
    return hi, mid, lo


def _tri(n):
    r = lax.broadcasted_iota(jnp.int32, (n, n), 0)
    c = lax.broadcasted_iota(jnp.int32, (n, n), 1)
    return jnp.where(c <= r, 1.0, 0.0).astype(bf16)


def _prefix_sum(t, x):
    hi, mid, lo = _split3(x)
    return _dot(t, hi) + _dot(t, mid) + _dot(t, lo)


def _log_sigmoid(z):
    return jnp.minimum(z, 0.0) - jnp.log1p(jnp.exp(-jnp.abs(z)))


def _resident(shape):
    nd = len(shape)
    return pl.BlockSpec(shape, lambda *_: (0,) * nd, pipeline_mode=pl.Buffered(1))


def _layer(shape, layer):
    nd = len(shape)
    return pl.BlockSpec((None,) + tuple(shape), lambda *_: (layer,) + (0,) * nd, pipeline_mode=pl.Buffered(1))


def _params(n_axes):
    return pltpu.CompilerParams(dimension_semantics=("arbitrary",) * n_axes,
                                vmem_limit_bytes=VMEM_LIMIT)


def _placement_constants():
    pk = np.zeros((3, LANES, N_HEADS * LANES), np.float32)
    ck_const = np.zeros((1, N_HEADS * LANES), np.float32)
    pq = np.zeros((3, LANES, LANES), np.float32)
    cq_const = np.zeros((1, LANES), np.float32)
    for j in range(3):
        cq_const[0, CK_LANE0 + j] = 1.0
        for h in range(N_HEADS):
            pk[j, h, h * LANES + CK_LANE0 + j] = -1.0
            ck_const[0, h * LANES + CQ_LANE0 + 16 * j + h] = 1.0
            pq[j, h, CQ_LANE0 + 16 * j + h] = 1.0
    expand = np.zeros((N_HEADS * 16, N_HEADS), np.float32)
    for h in range(N_HEADS):
        expand[h * 16:(h + 1) * 16, h] = 1.0
    head_sum = np.zeros((D_MODEL, LANES), np.float32)
    for h in range(N_HEADS):
        head_sum[h * HEAD_DIM:(h + 1) * HEAD_DIM, h] = 1.0
    return (jnp.asarray(pk, bf16), jnp.asarray(ck_const), jnp.asarray(pq, bf16),
            jnp.asarray(cq_const), jnp.asarray(expand, bf16), jnp.asarray(head_sum, bf16))


def _head_norm2(x, head_sum):
    xb = x.astype(bf16).astype(f32)
    return _dot((xb * xb).astype(bf16), head_sum) * NORM_SLACK


def _pack_heads(x, extras_of_head, out_ref):
    lane = lax.broadcasted_iota(jnp.int32, (x.shape[0], LANES), 1)
    for h in range(N_HEADS):
        pair = x[:, (h // 2) * LANES:(h // 2 + 1) * LANES]
        base = pair if h % 2 == 0 else pltpu.roll(pair, HEAD_DIM, axis=1)
        out_ref[h] = jnp.where(lane < HEAD_DIM, base, extras_of_head(h)).astype(bf16)


def _mixer_kernel(*refs, tm, carry_rows, seq_len):
    if carry_rows:
        x_ref, g_ref, win_ref, cw_ref, wout_ref, o_ref, u_ref, u_scr = refs
    else:
        x_ref, g_ref, win_ref, cw_ref, wout_ref, s1_ref, s2_ref, o_ref, u_ref, u_scr = refs

    @pl.when(pl.program_id(0) == 0)
    def _():
        u_scr[0:8, :] = jnp.zeros((8, D_MODEL), f32)

    x = x_ref[...]
    h = _rms(x, g_ref[...]).astype(bf16)
    bcx = _dot(h, win_ref[...])
    gb = bcx[:, 0:D_MODEL]
    u = bcx[:, D_MODEL:2 * D_MODEL] * bcx[:, 2 * D_MODEL:3 * D_MODEL]
    u_scr[8:8 + tm, :] = u
    u1 = u_scr[7:7 + tm, :]
    u2 = u_scr[6:6 + tm, :]
    if carry_rows:
        tail = u_scr[tm:tm + 8, :]
        u_scr[0:8, :] = tail
        u_ref[...] = tail
    else:
        pos = lax.broadcasted_iota(jnp.int32, (tm, 1), 0) % seq_len
        u1 = jnp.where(pos < 1, s1_ref[...], u1)
        u2 = jnp.where(pos < 2, s2_ref[...], u2)
        u_ref[...] = u
    cw = cw_ref[...]
    conv = cw[0:1, :] * u2 + cw[1:2, :] * u1 + cw[2:3, :] * u
    o_ref[...] = x + _dot((gb * conv).astype(bf16), wout_ref[...])


def _mixer(x, g, w_in, cw, w_out, layer, state_rows=None, seq_len=None):
    n = x.shape[0]
    tm = min(ROW_TILE, n)
    carry = state_rows is None
    row = pl.BlockSpec((tm, D_MODEL), lambda i: (i, 0))
    in_specs = [row, _resident((1, D_MODEL)), _layer((D_MODEL, 3 * D_MODEL), layer),
                _layer((3, D_MODEL), layer), _layer((D_MODEL, D_MODEL), layer)]
    args = [x, g, w_in, cw, w_out]
    if carry:
        u_spec = pl.BlockSpec((8, D_MODEL), lambda i: (0, 0))
        u_shape = jax.ShapeDtypeStruct((8, D_MODEL), f32)
    else:
        assert n == tm
        in_specs += [row, row]
        args += list(state_rows)
        u_spec = row
        u_shape = jax.ShapeDtypeStruct((n, D_MODEL), f32)
    return pl.pallas_call(
        functools.partial(_mixer_kernel, tm=tm, carry_rows=carry, seq_len=seq_len),
        grid=(n // tm,),
        in_specs=in_specs,
        out_specs=[row, u_spec],
        out_shape=[jax.ShapeDtypeStruct((n, D_MODEL), f32), u_shape],
        scratch_shapes=[pltpu.VMEM((tm + 8, D_MODEL), f32)],
        compiler_params=_params(1),
        name="mixer",
    )(*args)


def _mlp_kernel(*refs, has_attn, final_norm):
    refs = list(refs)
    x_ref = refs.pop(0)
    if has_attn:
        attn_ref = refs.pop(0)
        wo_ref = refs.pop(0)
    g_ref, wup_ref, wdn_ref = refs[0:3]
    gf_ref = refs[3] if final_norm else None
    o_ref = refs[-1]

    x = x_ref[...]
    if has_attn:
        x = x + _dot(attn_ref[...], wo_ref[...])
    h = _rms(x, g_ref[...]).astype(bf16)
    a = jnp.maximum(_dot(h, wup_ref[...]), 0.0)
    y = x + _dot((a * a).astype(bf16), wdn_ref[...])
    if final_norm:
        y = _rms(y, gf_ref[...])
    o_ref[...] = y


def _mlp(x, g, w_up, w_down, layer, attn=None, w_o=None, attn_layer=None, g_final=None):
    n = x.shape[0]
    tm = min(ROW_TILE, n)
    row = pl.BlockSpec((tm, D_MODEL), lambda i: (i, 0))
    in_specs, args = [row], [x]
    if attn is not None:
        in_specs += [row, _layer((D_MODEL, D_MODEL), attn_layer)]
        args += [attn, w_o]
    in_specs += [_resident((1, D_MODEL)), _layer((D_MODEL, D_FF), layer), _layer((D_FF, D_MODEL), layer)]
    args += [g, w_up, w_down]
    if g_final is not None:
        in_specs.append(_resident((1, D_MODEL)))
        args.append(g_final)
    return pl.pallas_call(
        functools.partial(_mlp_kernel, has_attn=attn is not None, final_norm=g_final is not None),
        grid=(n // tm,),
        in_specs=in_specs,
        out_specs=row,
        out_shape=jax.ShapeDtypeStruct((n, D_MODEL), f32),
        compiler_params=_params(1),
        name="mlp",
    )(*args)


def _kv_kernel(*refs, tm, pack):
    if pack:
        (x_ref, g_ref, wk_ref, wv_ref, wf_ref, bf_ref, pk_ref, ckc_ref, hs_ref,
         k_ref, v_ref, lf_ref, c_ref, kp_ref, vt_ref, n2_ref, carry) = refs
    else:
        x_ref, g_ref, wk_ref, wv_ref, wf_ref, bf_ref, k_ref, v_ref, lf_ref = refs

    hk = _rms(x_ref[...], g_ref[...]).astype(bf16)
    k = _dot(hk, wk_ref[...])
    v = _dot(hk, wv_ref[...])
    z = _dot(hk, wf_ref[...]) + bf_ref[...]
    lane = lax.broadcasted_iota(jnp.int32, (tm, LANES), 1)
    logf = jnp.where(lane < N_HEADS, _log_sigmoid(z), 0.0)
    k_ref[...] = k
    v_ref[...] = v
    lf_ref[...] = logf[:, 0:N_HEADS]
    if not pack:
        return

    @pl.when(pl.program_id(0) == 0)
    def _():
        carry[...] = jnp.zeros_like(carry)

    c = _prefix_sum(_tri(tm), logf) + carry[0:1, :]
    carry[0:1, :] = c[tm - 1:tm, :]
    c2 = c * LOG2E
    c_ref[...] = c2
    hi, mid, lo = _split3(c2)
    extras = _dot(hi, pk_ref[0]) + _dot(mid, pk_ref[1]) + _dot(lo, pk_ref[2]) + ckc_ref[...]
    _pack_heads(k, lambda h: extras[:, h * LANES:(h + 1) * LANES], kp_ref)
    n2_ref[...] = _head_norm2(k, hs_ref[...])
    pad_row = lax.broadcasted_iota(jnp.int32, (VT_ROWS - HEAD_DIM, tm), 0)
    ones_row = jnp.where(pad_row == 0, 1.0, 0.0)
    for p in range(N_HEADS // 2):
        vt = jnp.transpose(v[:, p * LANES:(p + 1) * LANES])
        for hh in range(2):
            rows = vt[hh * HEAD_DIM:(hh + 1) * HEAD_DIM, :]
            vt_ref[2 * p + hh, 0] = jnp.concatenate([rows, ones_row], axis=0).astype(bf16)


def _kv(x, g, w_k, w_v, w_f, b_f, pk=None, ck_const=None, head_sum=None):
    n = x.shape[0]
    tm = min(ROW_TILE, n)
    pack = pk is not None
    row = pl.BlockSpec((tm, D_MODEL), lambda i: (i, 0))
    heads = pl.BlockSpec((N_HEADS, tm, LANES), lambda i: (0, i, 0))
    in_specs = [row, _resident((1, D_MODEL)), _resident((D_MODEL, D_MODEL)), _resident((D_MODEL, D_MODEL)),
                _resident((D_MODEL, LANES)), _resident((1, LANES))]
    args = [x, g, w_k, w_v, w_f, b_f]
    out_specs = [row, row, pl.BlockSpec((tm, N_HEADS), lambda i: (i, 0))]
    out_shape = [jax.ShapeDtypeStruct((n, D_MODEL), f32), jax.ShapeDtypeStruct((n, D_MODEL), f32),
                 jax.ShapeDtypeStruct((n, N_HEADS), f32)]
    scratch = []
    if pack:
        lanes = pl.BlockSpec((tm, LANES), lambda i: (i, 0))
        in_specs += [_resident((3, LANES, N_HEADS * LANES)), _resident((1, N_HEADS * LANES)),
                     _resident((D_MODEL, LANES))]
        args += [pk, ck_const, head_sum]
        out_specs += [lanes, heads, pl.BlockSpec((N_HEADS, 1, VT_ROWS, tm), lambda i: (0, i, 0, 0)), lanes]
        out_shape += [jax.ShapeDtypeStruct((n, LANES), f32),
                      jax.ShapeDtypeStruct((N_HEADS, n, LANES), bf16),
                      jax.ShapeDtypeStruct((N_HEADS, n // tm, VT_ROWS, tm), bf16),
                      jax.ShapeDtypeStruct((n, LANES), f32)]
        scratch = [pltpu.VMEM((8, LANES), f32)]
    return pl.pallas_call(
        functools.partial(_kv_kernel, tm=tm, pack=pack),
        grid=(n // tm,),
        in_specs=in_specs,
        out_specs=out_specs,
        out_shape=out_shape,
        scratch_shapes=scratch,
        compiler_params=_params(1),
        name="kv",
    )(*args)


def _qproj_kernel(*refs, pack):
    if pack:
        x_ref, g_ref, wq_ref, c_ref, pq_ref, cqc_ref, hs_ref, o_ref, n2_ref = refs
    else:
        x_ref, g_ref, wq_ref, o_ref = refs
    h = _rms(x_ref[...], g_ref[...]).astype(bf16)
    q = _dot(h, wq_ref[...]) * Q_SCALE
    if not pack:
        o_ref[...] = q
        return
    hi, mid, lo = _split3(c_ref[...])
    extras = _dot(hi, pq_ref[0]) + _dot(mid, pq_ref[1]) + _dot(lo, pq_ref[2]) + cqc_ref[...]
    _pack_heads(q, lambda h: extras, o_ref)
    n2_ref[...] = _head_norm2(q, hs_ref[...])


def _qproj(x, g, w_q, layer, c2=None, pq=None, cq_const=None, head_sum=None):
    n = x.shape[0]
    tm = min(ROW_TILE, n)
    pack = c2 is not None
    row = pl.BlockSpec((tm, D_MODEL), lambda i: (i, 0))
    in_specs = [row, _resident((1, D_MODEL)), _layer((D_MODEL, D_MODEL), layer)]
    args = [x, g, w_q]
    if pack:
        lanes = pl.BlockSpec((tm, LANES), lambda i: (i, 0))
        in_specs += [lanes, _resident((3, LANES, LANES)), _resident((1, LANES)), _resident((D_MODEL, LANES))]
        args += [c2, pq, cq_const, head_sum]
        out_spec = [pl.BlockSpec((N_HEADS, tm, LANES), lambda i: (0, i, 0)), lanes]
        out_shape = [jax.ShapeDtypeStruct((N_HEADS, n, LANES), bf16), jax.ShapeDtypeStruct((n, LANES), f32)]
    else:
        out_spec = row
        out_shape = jax.ShapeDtypeStruct((n, D_MODEL), f32)
    return pl.pallas_call(
        functools.partial(_qproj_kernel, pack=pack),
        grid=(n // tm,),
        in_specs=in_specs,
        out_specs=out_spec,
        out_shape=out_shape,
        compiler_params=_params(1),
        name="qproj",
    )(*args)


def _first_needed_tile(qn2, kn2, c2, t):
    n = qn2.shape[0]
    tq = 2 * t
    nq, nk = n // tq, n // t
    qn, kn, c = jnp.sqrt(qn2[:, :N_HEADS]), jnp.sqrt(kn2[:, :N_HEADS]), c2[:, :N_HEADS]
    q_max = qn.reshape(nq, tq, N_HEADS).max(axis=1)
    k_max = kn.reshape(nk, t, N_HEADS).max(axis=1)
    k_run = lax.cummax(k_max, axis=0)
    cq_max = c.reshape(nq, tq, N_HEADS).max(axis=1)
    ck_min = c.reshape(nk, t, N_HEADS).min(axis=1)
    m_low = -q_max * k_max.reshape(nq, 2, N_HEADS).max(axis=1)
    upper = q_max[:, None, :] * k_run[None, :, :] + cq_max[:, None, :] - ck_min[None, :, :]
    older = jnp.arange(nk)[None, :, None] < 2 * jnp.arange(nq)[:, None, None]
    dead = (upper <= m_low[:, None, :] - SKIP_MARGIN) & older
    lead = jnp.cumprod(dead.astype(jnp.int32), axis=1).sum(axis=1)
    first = lead.reshape(nq, N_HEADS // 2, 2).min(axis=2)
    return ((first // 2) * 2).T.astype(jnp.int32)


def _flash_kernel(first_ref, q_ref, k_ref, vt_ref, o_ref, m_scr, acc_scr, s_even, s_odd, *, t):
    tq = 2 * t
    i = pl.program_id(1)
    first = first_ref[pl.program_id(0), i]
    for hh in range(2):
        m_scr[hh] = jnp.full((1, tq), NEG_BIG, f32)
        acc_scr[hh] = jnp.zeros((VT_ROWS, tq), f32)

    def logits(j, hh, c0):
        start = pl.multiple_of(j * t, t)
        q = q_ref[hh, c0:c0 + QUERY_CHUNK, :]
        return _dot_nt(k_ref[hh, pl.ds(start, t), :], q)

    def step(j, j_next, s_cur, s_next, diag_offset=None):
        for hh in range(2):
            for c0 in range(0, tq, QUERY_CHUNK):
                cols = slice(c0, c0 + QUERY_CHUNK)
                s = s_cur[hh, :, cols]
                s_next[hh, :, cols] = logits(j_next, hh, c0)
                if diag_offset is not None:
                    key = lax.broadcasted_iota(jnp.int32, (t, QUERY_CHUNK), 0) + (diag_offset - c0)
                    qry = lax.broadcasted_iota(jnp.int32, (t, QUERY_CHUNK), 1)
                    s = jnp.where(key <= qry, s, NEG_BIG)
                m_prev = m_scr[hh, :, cols]
                m_new = jnp.maximum(m_prev, jnp.max(s, axis=0, keepdims=True))
                p = jnp.exp2(s - m_new)
                pv = _dot(vt_ref[hh, j], p.astype(bf16))
                acc_scr[hh, :, cols] = jnp.exp2(m_prev - m_new) * acc_scr[hh, :, cols] + pv
                m_scr[hh, :, cols] = m_new

    for hh in range(2):
        for c0 in range(0, tq, QUERY_CHUNK):
            s_even[hh, :, c0:c0 + QUERY_CHUNK] = logits(2 * i, hh, c0)
    step(2 * i, 2 * i + 1, s_even, s_odd, diag_offset=0)
    step(2 * i + 1, jnp.maximum(2 * i - 1, 0), s_odd, s_even, diag_offset=t)

    def pair(n, carry):
        j = 2 * i - 1 - 2 * n
        step(j, j - 1, s_even, s_odd)
        step(j - 1, jnp.maximum(j - 2, 0), s_odd, s_even)
        return carry

    lax.fori_loop(0, (2 * i - first) // 2, pair, 0)

    outs = []
    for hh in range(2):
        acc = acc_scr[hh]
        outs.append(acc[0:HEAD_DIM, :] / acc[HEAD_DIM:HEAD_DIM + 1, :])
    o_ref[...] = jnp.transpose(jnp.concatenate(outs, axis=0)).astype(bf16)


def _flash(qp, kp, vt, first):
    n = qp.shape[1]
    t = ATT_TILE
    tq = 2 * t
    assert vt.shape == (N_HEADS, n // t, VT_ROWS, t) and n % tq == 0
    assert first.shape == (N_HEADS // 2, n // tq)
    grid_spec = pltpu.PrefetchScalarGridSpec(
        num_scalar_prefetch=1,
        grid=(N_HEADS // 2, n // tq),
        in_specs=[pl.BlockSpec((2, tq, LANES), lambda p, i, first_ref: (p, i, 0)),
                  pl.BlockSpec((2, n, LANES), lambda p, i, first_ref: (p, 0, 0)),
                  pl.BlockSpec((2, n // t, VT_ROWS, t), lambda p, i, first_ref: (p, 0, 0, 0))],
        out_specs=pl.BlockSpec((tq, LANES), lambda p, i, first_ref: (i, p)),
        scratch_shapes=[pltpu.VMEM((2, 1, tq), f32), pltpu.VMEM((2, VT_ROWS, tq), f32),
                        pltpu.VMEM((2, t, tq), f32), pltpu.VMEM((2, t, tq), f32)])
    return pl.pallas_call(
        functools.partial(_flash_kernel, t=t),
        grid_spec=grid_spec,
        out_shape=jax.ShapeDtypeStruct((n, D_MODEL), bf16),
        compiler_params=_params(2),
        name="flash",
    )(first, qp, kp, vt)


def _cum_kernel(past_ref, new_ref, o_ref, *, past_len, new_len, chunk):
    t = _tri(chunk)
    carry = jnp.zeros((1, N_HEADS), f32)
    for s in range(0, past_len, chunk):
        c = _prefix_sum(t, past_ref[0, s:s + chunk, :]) + carry
        o_ref[0, s:s + chunk, :] = c * LOG2E
        carry = c[chunk - 1:chunk, :]
    c = _prefix_sum(_tri(new_len), new_ref[0]) + carry
    o_ref[0, past_len:past_len + new_len, :] = c * LOG2E


def _cum(past_logf, new_logf):
    b, past_len, _ = past_logf.shape
    new_len = new_logf.shape[1]
    total = past_len + new_len
    return pl.pallas_call(
        functools.partial(_cum_kernel, past_len=past_len, new_len=new_len, chunk=512),
        grid=(b,),
        in_specs=[pl.BlockSpec((1, past_len, N_HEADS), lambda i: (i, 0, 0)),
                  pl.BlockSpec((1, new_len, N_HEADS), lambda i: (i, 0, 0))],
        out_specs=pl.BlockSpec((1, total, N_HEADS), lambda i: (i, 0, 0)),
        out_shape=jax.ShapeDtypeStruct((b, total, N_HEADS), f32),
        compiler_params=_params(1),
        name="cum",
    )(past_logf, new_logf)


def _expand_heads(e, x):
    hi, mid, lo = _split3(x)
    return _dot_nt(e, hi) + _dot_nt(e, mid) + _dot_nt(e, lo)


def _decode_attn_kernel(q_ref, kc_ref, vc_ref, kn_ref, vn_ref, c_ref, e_ref, o_ref, *, past_len, new_len):
    rows = N_HEADS * new_len
    q = q_ref[0]
    row_head = lax.broadcasted_iota(jnp.int32, (rows, D_MODEL), 0) // new_len
    col_head = lax.broadcasted_iota(jnp.int32, (rows, D_MODEL), 1) // HEAD_DIM
    qbd = jnp.where(row_head == col_head, jnp.tile(q, (N_HEADS, 1)), 0.0).astype(bf16)
    e = e_ref[...]
    c2 = c_ref[0]
    ck_new = _expand_heads(e, c2[past_len:past_len + new_len, :])
    t_of_row = lax.broadcasted_iota(jnp.int32, (rows, new_len), 0) % new_len
    s_idx = lax.broadcasted_iota(jnp.int32, (rows, new_len), 1)
    cq = jnp.sum(jnp.where(s_idx == t_of_row, ck_new, 0.0), axis=1, keepdims=True)

    lg = _dot_nt(qbd, kc_ref[0]) + cq - _expand_heads(e, c2[0:past_len, :])
    lg_new = jnp.where(s_idx <= t_of_row, _dot_nt(qbd, kn_ref[0].astype(bf16)) + cq - ck_new, NEG_BIG)
    m = jnp.maximum(jnp.max(lg, axis=1, keepdims=True), jnp.max(lg_new, axis=1, keepdims=True))
    p = jnp.exp2(lg - m)
    p_new = jnp.exp2(lg_new - m)
    denom = jnp.sum(p, axis=1, keepdims=True) + jnp.sum(p_new, axis=1, keepdims=True)
    o = (_dot(p.astype(bf16), vc_ref[0])
         + _dot(p_new.astype(bf16), vn_ref[0].astype(bf16))) / denom
    lane = lax.broadcasted_iota(jnp.int32, (new_len, LANES), 1)
    for pp in range(N_HEADS // 2):
        even = o[(2 * pp) * new_len:(2 * pp + 1) * new_len, pp * LANES:(pp + 1) * LANES]
        odd = o[(2 * pp + 1) * new_len:(2 * pp + 2) * new_len, pp * LANES:(pp + 1) * LANES]
        o_ref[0, :, pp * LANES:(pp + 1) * LANES] = jnp.where(lane < HEAD_DIM, even, odd).astype(bf16)


def _decode_attn(q, cache_k, cache_v, k_new, v_new, c2, expand):
    b, past_len, _ = cache_k.shape
    new_len = q.shape[1]
    cache = pl.BlockSpec((1, past_len, D_MODEL), lambda i: (i, 0, 0))
    new = pl.BlockSpec((1, new_len, D_MODEL), lambda i: (i, 0, 0))
    return pl.pallas_call(
        functools.partial(_decode_attn_kernel, past_len=past_len, new_len=new_len),
        grid=(b,),
        in_specs=[new, cache, cache, new, new,
                  pl.BlockSpec((1, past_len + new_len, N_HEADS), lambda i: (i, 0, 0)),
                  _resident((N_HEADS * new_len, N_HEADS))],
        out_specs=new,
        out_shape=jax.ShapeDtypeStruct((b, new_len, D_MODEL), bf16),
        compiler_params=_params(1),
        name="decode_attn",
    )(q, cache_k, cache_v, k_new, v_new, c2, expand)


def kernel(x_prompt, x_sample, cache_k, cache_v, cache_logf, state_conv, norm_mix, norm_mlp, conv_in, conv_w,
           conv_out, norm_kv, w_k, w_v, w_f, b_f, w_q, w_o, mlp_up, mlp_down, norm_final):
    assert x_prompt.shape[0] == 1 and x_prompt.shape[2] == D_MODEL
    seq = x_prompt.shape[1]
    dec_b, dec_s, _ = x_sample.shape
    past_len = cache_k.shape[1]
    assert seq % ROW_TILE == 0 and ROW_TILE == ATT_TILE and (dec_b * dec_s) % 8 == 0

    pk, ck_const, pq, cq_const, expand, head_sum = _placement_constants()
    row = lambda a: a.reshape(1, -1)
    conv_in_b, conv_out_b = conv_in.astype(bf16), conv_out.astype(bf16)
    w_k_b, w_v_b, w_q_b, w_o_b = w_k.astype(bf16), w_v.astype(bf16), w_q.astype(bf16), w_o.astype(bf16)
    up_b, down_b = mlp_up.astype(bf16), mlp_down.astype(bf16)
    w_f_b = jnp.pad(w_f, ((0, 0), (0, LANES - N_HEADS))).astype(bf16)
    b_f_p = jnp.pad(b_f, (0, LANES - N_HEADS)).reshape(1, LANES)

    def mlp(x, l, **kw):
        return _mlp(x, row(norm_mlp[l]), up_b, down_b, l, **kw)

    x = x_prompt.reshape(seq, D_MODEL)
    tails = []
    for l in range(N_A):
        x, tail = _mixer(x, row(norm_mix[l]), conv_in_b, conv_w, conv_out_b, l)
        tails.append(tail[6:8])
        x = mlp(x, l)
    k_p, v_p, logf_p, c2_p, kp, vt, kn2 = _kv(x, row(norm_kv), w_k_b, w_v_b, w_f_b, b_f_p, pk, ck_const, head_sum)
    for l in range(N_A, DEPTH):
        j = l - N_A
        qp, qn2 = _qproj(x, row(norm_mix[l]), w_q_b, j, c2_p, pq, cq_const, head_sum)
        attn = _flash(qp, kp, vt, _first_needed_tile(qn2, kn2, c2_p, ATT_TILE))
        x = mlp(x, l, attn=attn, w_o=w_o_b, attn_layer=j,
                g_final=row(norm_final) if l == DEPTH - 1 else None)
    y_prompt = x.reshape(1, seq, D_MODEL)
    conv_prompt = jnp.stack(tails).reshape(N_A, 1, 2, D_MODEL)

    n_s = dec_b * dec_s
    x = x_sample.reshape(n_s, D_MODEL)
    new_states = []
    for l in range(N_A):
        st = state_conv[l]
        pad = jnp.zeros((dec_b, dec_s - 2, D_MODEL), f32)
        s2 = jnp.concatenate([st, pad], axis=1).reshape(n_s, D_MODEL)
        s1 = jnp.concatenate([st[:, 1:2], pad, pad[:, 0:1]], axis=1).reshape(n_s, D_MODEL)
        x, u = _mixer(x, row(norm_mix[l]), conv_in_b, conv_w, conv_out_b, l,
                      state_rows=(s1, s2), seq_len=dec_s)
        new_states.append(u.reshape(dec_b, dec_s, D_MODEL)[:, dec_s - 2:])
        x = mlp(x, l)
    k_s, v_s, logf_s = _kv(x, row(norm_kv), w_k_b, w_v_b, w_f_b, b_f_p)
    c2_s = _cum(cache_logf, logf_s.reshape(dec_b, dec_s, N_HEADS))
    k_s3, v_s3 = k_s.reshape(dec_b, dec_s, D_MODEL), v_s.reshape(dec_b, dec_s, D_MODEL)
    cache_k3 = cache_k.astype(bf16).reshape(dec_b, past_len, D_MODEL)
    cache_v3 = cache_v.astype(bf16).reshape(dec_b, past_len, D_MODEL)
    for l in range(N_A, DEPTH):
        j = l - N_A
        q = _qproj(x, row(norm_mix[l]), w_q_b, j)
        attn = _decode_attn(q.reshape(dec_b, dec_s, D_MODEL), cache_k3, cache_v3, k_s3, v_s3, c2_s, expand)
        x = mlp(x, l, attn=attn.reshape(n_s, D_MODEL), w_o=w_o_b, attn_layer=j,
                g_final=row(norm_final) if l == DEPTH - 1 else None)
    y_sample = x.reshape(dec_b, dec_s, D_MODEL)

    return (y_prompt, y_sample,
            k_p.reshape(1, seq, N_HEADS, HEAD_DIM), v_p.reshape(1, seq, N_HEADS, HEAD_DIM),
            logf_p.reshape(1, seq, N_HEADS), conv_prompt,
            k_s.reshape(dec_b, dec_s, N_HEADS, HEAD_DIM), v_s.reshape(dec_b, dec_s, N_HEADS, HEAD_DIM),
            logf_s.reshape(dec_b, dec_s, N_HEADS), jnp.stack(new_states))
```

```python
import functools
import math

import numpy as np
import jax
import jax.numpy as jnp
from jax import lax
from jax.experimental import pallas as pl
from jax.experimental.pallas import tpu as pltpu

D_MODEL = 1024
N_HEADS = 16
HEAD_DIM = 64
D_FF = 4 * D_MODEL
N_A = 2
DEPTH = 4
RMS_EPS = 1e-6
LOG2E = math.log2(math.e)
Q_SCALE = HEAD_DIM ** -0.5 * LOG2E
NEG_BIG = -1e30

LANES = 128
ROW_TILE = 512
ATT_TILE = 512
QUERY_CHUNK = 256
SKIP_MARGIN = 170.0
NORM_SLACK = 1.01
VMEM_LIMIT = 56 * 1024 * 1024

CQ_LANE0 = 64
CK_LANE0 = 112
VT_ROWS = 80

bf16 = jnp.bfloat16
f32 = jnp.float32


def _dot(a, b):
    return jnp.dot(a, b, preferred_element_type=f32)


def _dot_nt(a, b):
    return lax.dot_general(a, b, (((1,), (1,)), ((), ())), preferred_element_type=f32)


def _rms(x, g):
    ms = jnp.mean(x * x, axis=-1, keepdims=True)
    return x * lax.rsqrt(ms + RMS_EPS) * g


def _split3(x):
    hi = x.astype(bf16)
    r1 = x - hi.astype(f32)
    mid = r1.astype(bf16)
    lo = (r1 - mid.astype(f32)).astype(bf16)
    return hi, mid, lo


def _tri(n):
    r = lax.broadcasted_iota(jnp.int32, (n, n), 0)
    c = lax.broadcasted_iota(jnp.int32, (n, n), 1)
    return jnp.where(c <= r, 1.0, 0.0).astype(bf16)


def _prefix_sum(t, x):
    hi, mid, lo = _split3(x)
    return _dot(t, hi) + _dot(t, mid) + _dot(t, lo)


def _log_sigmoid(z):
    return jnp.minimum(z, 0.0) - jnp.log1p(jnp.exp(-jnp.abs(z)))


def _resident(shape):
    nd = len(shape)
    return pl.BlockSpec(shape, lambda *_: (0,) * nd, pipeline_mode=pl.Buffered(1))


def _layer(shape, layer):
    nd = len(shape)
    return pl.BlockSpec((None,) + tuple(shape), lambda *_: (layer,) + (0,) * nd, pipeline_mode=pl.Buffered(1))


def _params(n_axes):
    return pltpu.CompilerParams(dimension_semantics=("arbitrary",) * n_axes,
                                vmem_limit_bytes=VMEM_LIMIT)


def _placement_constants():
    pk = np.zeros((3, LANES, N_HEADS * LANES), np.float32)
    ck_const = np.zeros((1, N_HEADS * LANES), np.float32)
    pq = np.zeros((3, LANES, LANES), np.float32)
    cq_const = np.zeros((1, LANES), np.float32)
    for j in range(3):
        cq_const[0, CK_LANE0 + j] = 1.0
        for h in range(N_HEADS):
            pk[j, h, h * LANES + CK_LANE0 + j] = -1.0
            ck_const[0, h * LANES + CQ_LANE0 + 16 * j + h] = 1.0
            pq[j, h, CQ_LANE0 + 16 * j + h] = 1.0
    expand = np.zeros((N_HEADS * 16, N_HEADS), np.float32)
    for h in range(N_HEADS):
        expand[h * 16:(h + 1) * 16, h] = 1.0
    head_sum = np.zeros((D_MODEL, LANES), np.float32)
    for h in range(N_HEADS):
        head_sum[h * HEAD_DIM:(h + 1) * HEAD_DIM, h] = 1.0
    return (jnp.asarray(pk, bf16), jnp.asarray(ck_const), jnp.asarray(pq, bf16),
            jnp.asarray(cq_const), jnp.asarray(expand, bf16), jnp.asarray(head_sum, bf16))


def _head_norm2(x, head_sum):
    xb = x.astype(bf16).astype(f32)
    return _dot((xb * xb).astype(bf16), head_sum) * NORM_SLACK


def _pack_heads(x, extras_of_head, out_ref):
    lane = lax.broadcasted_iota(jnp.int32, (x.shape[0], LANES), 1)
    for h in range(N_HEADS):
        pair = x[:, (h // 2) * LANES:(h // 2 + 1) * LANES]
        base = pair if h % 2 == 0 else pltpu.roll(pair, HEAD_DIM, axis=1)
        out_ref[h] = jnp.where(lane < HEAD_DIM, base, extras_of_head(h)).astype(bf16)


def _mixer_kernel(*refs, tm, carry_rows, seq_len):
    if carry_rows:
        x_ref, g_ref, win_ref, cw_ref, wout_ref, o_ref, u_ref, u_scr = refs
    else:
        x_ref, g_ref, win_ref, cw_ref, wout_ref, s1_ref, s2_ref, o_ref, u_ref, u_scr = refs

    @pl.when(pl.program_id(0) == 0)
    def _():
        u_scr[0:8, :] = jnp.zeros((8, D_MODEL), f32)

    x = x_ref[...]
    h = _rms(x, g_ref[...]).astype(bf16)
    bcx = _dot(h, win_ref[...])
    gb = bcx[:, 0:D_MODEL]
    u = bcx[:, D_MODEL:2 * D_MODEL] * bcx[:, 2 * D_MODEL:3 * D_MODEL]
    u_scr[8:8 + tm, :] = u
    u1 = u_scr[7:7 + tm, :]
    u2 = u_scr[6:6 + tm, :]
    if carry_rows:
        tail = u_scr[tm:tm + 8, :]
        u_scr[0:8, :] = tail
        u_ref[...] = tail
    else:
        pos = lax.broadcasted_iota(jnp.int32, (tm, 1), 0) % seq_len
        u1 = jnp.where(pos < 1, s1_ref[...], u1)
        u2 = jnp.where(pos < 2, s2_ref[...], u2)
        u_ref[...] = u
    cw = cw_ref[...]
    conv = cw[0:1, :] * u2 + cw[1:2, :] * u1 + cw[2:3, :] * u
    o_ref[...] = x + _dot((gb * conv).astype(bf16), wout_ref[...])


def _mixer(x, g, w_in, cw, w_out, layer, state_rows=None, seq_len=None):
    n = x.shape[0]
    tm = min(ROW_TILE, n)
    carry = state_rows is None
    row = pl.BlockSpec((tm, D_MODEL), lambda i: (i, 0))
    in_specs = [row, _resident((1, D_MODEL)), _layer((D_MODEL, 3 * D_MODEL), layer),
                _layer((3, D_MODEL), layer), _layer((D_MODEL, D_MODEL), layer)]
    args = [x, g, w_in, cw, w_out]
    if carry:
        u_spec = pl.BlockSpec((8, D_MODEL), lambda i: (0, 0))
        u_shape = jax.ShapeDtypeStruct((8, D_MODEL), f32)
    else:
        assert n == tm
        in_specs += [row, row]
        args += list(state_rows)
        u_spec = row
        u_shape = jax.ShapeDtypeStruct((n, D_MODEL), f32)
    return pl.pallas_call(
        functools.partial(_mixer_kernel, tm=tm, carry_rows=carry, seq_len=seq_len),
        grid=(n // tm,),
        in_specs=in_specs,
        out_specs=[row, u_spec],
        out_shape=[jax.ShapeDtypeStruct((n, D_MODEL), f32), u_shape],
        scratch_shapes=[pltpu.VMEM((tm + 8, D_MODEL), f32)],
        compiler_params=_params(1),
        name="mixer",
    )(*args)


def _mlp_kernel(*refs, has_attn, final_norm):
    refs = list(refs)
    x_ref = refs.pop(0)
    if has_attn:
        attn_ref = refs.pop(0)
        wo_ref = refs.pop(0)
    g_ref, wup_ref, wdn_ref = refs[0:3]
    gf_ref = refs[3] if final_norm else None
    o_ref = refs[-1]

    x = x_ref[...]
    if has_attn:
        x = x + _dot(attn_ref[...], wo_ref[...])
    h = _rms(x, g_ref[...]).astype(bf16)
    a = jnp.maximum(_dot(h, wup_ref[...]), 0.0)
    y = x + _dot((a * a).astype(bf16), wdn_ref[...])
    if final_norm:
        y = _rms(y, gf_ref[...])
    o_ref[...] = y


def _mlp(x, g, w_up, w_down, layer, attn=None, w_o=None, attn_layer=None, g_final=None):
    n = x.shape[0]
    tm = min(ROW_TILE, n)
    row = pl.BlockSpec((tm, D_MODEL), lambda i: (i, 0))
    in_specs, args = [row], [x]
    if attn is not None:
        in_specs += [row, _layer((D_MODEL, D_MODEL), attn_layer)]
        args += [attn, w_o]
    in_specs += [_resident((1, D_MODEL)), _layer((D_MODEL, D_FF), layer), _layer((D_FF, D_MODEL), layer)]
    args += [g, w_up, w_down]
    if g_final is not None:
        in_specs.append(_resident((1, D_MODEL)))
        args.append(g_final)
    return pl.pallas_call(
        functools.partial(_mlp_kernel, has_attn=attn is not None, final_norm=g_final is not None),
        grid=(n // tm,),
        in_specs=in_specs,
        out_specs=row,
        out_shape=jax.ShapeDtypeStruct((n, D_MODEL), f32),
        compiler_params=_params(1),
        name="mlp",
    )(*args)


def _kv_kernel(*refs, tm, pack):
    if pack:
        (x_ref, g_ref, wk_ref, wv_ref, wf_ref, bf_ref, pk_ref, ckc_ref, hs_ref,
         k_ref, v_ref, lf_ref, c_ref, kp_ref, vt_ref, n2_ref, carry) = refs
    else:
        x_ref, g_ref, wk_ref, wv_ref, wf_ref, bf_ref, k_ref, v_ref, lf_ref = refs

    hk = _rms(x_ref[...], g_ref[...]).astype(bf16)
    k = _dot(hk, wk_ref[...])
    v = _dot(hk, wv_ref[...])
    z = _dot(hk, wf_ref[...]) + bf_ref[...]
    lane = lax.broadcasted_iota(jnp.int32, (tm, LANES), 1)
    logf = jnp.where(lane < N_HEADS, _log_sigmoid(z), 0.0)
    k_ref[...] = k
    v_ref[...] = v
    lf_ref[...] = logf[:, 0:N_HEADS]
    if not pack:
        return

    @pl.when(pl.program_id(0) == 0)
    def _():
        carry[...] = jnp.zeros_like(carry)

    c = _prefix_sum(_tri(tm), logf) + carry[0:1, :]
    carry[0:1, :] = c[tm - 1:tm, :]
    c2 = c * LOG2E
    c_ref[...] = c2
    hi, mid, lo = _split3(c2)
    extras = _dot(hi, pk_ref[0]) + _dot(mid, pk_ref[1]) + _dot(lo, pk_ref[2]) + ckc_ref[...]
    _pack_heads(k, lambda h: extras[:, h * LANES:(h + 1) * LANES], kp_ref)
    n2_ref[...] = _head_norm2(k, hs_ref[...])
    pad_row = lax.broadcasted_iota(jnp.int32, (VT_ROWS - HEAD_DIM, tm), 0)
    ones_row = jnp.where(pad_row == 0, 1.0, 0.0)
    for p in range(N_HEADS // 2):
        vt = jnp.transpose(v[:, p * LANES:(p + 1) * LANES])
        for hh in range(2):
            rows = vt[hh * HEAD_DIM:(hh + 1) * HEAD_DIM, :]
            vt_ref[2 * p + hh, 0] = jnp.concatenate([rows, ones_row], axis=0).astype(bf16)


def _kv(x, g, w_k, w_v, w_f, b_f, pk=None, ck_const=None, head_sum=None):
    n = x.shape[0]
    tm = min(ROW_TILE, n)
    pack = pk is not None
    row = pl.BlockSpec((tm, D_MODEL), lambda i: (i, 0))
    heads = pl.BlockSpec((N_HEADS, tm, LANES), lambda i: (0, i, 0))
    in_specs = [row, _resident((1, D_MODEL)), _resident((D_MODEL, D_MODEL)), _resident((D_MODEL, D_MODEL)),
                _resident((D_MODEL, LANES)), _resident((1, LANES))]
    args = [x, g, w_k, w_v, w_f, b_f]
    out_specs = [row, row, pl.BlockSpec((tm, N_HEADS), lambda i: (i, 0))]
    out_shape = [jax.ShapeDtypeStruct((n, D_MODEL), f32), jax.ShapeDtypeStruct((n, D_MODEL), f32),
                 jax.ShapeDtypeStruct((n, N_HEADS), f32)]
    scratch = []
    if pack:
        lanes = pl.BlockSpec((tm, LANES), lambda i: (i, 0))
        in_specs += [_resident((3, LANES, N_HEADS * LANES)), _resident((1, N_HEADS * LANES)),
                     _resident((D_MODEL, LANES))]
        args += [pk, ck_const, head_sum]
        out_specs += [lanes, heads, pl.BlockSpec((N_HEADS, 1, VT_ROWS, tm), lambda i: (0, i, 0, 0)), lanes]
        out_shape += [jax.ShapeDtypeStruct((n, LANES), f32),
                      jax.ShapeDtypeStruct((N_HEADS, n, LANES), bf16),
                      jax.ShapeDtypeStruct((N_HEADS, n // tm, VT_ROWS, tm), bf16),
                      jax.ShapeDtypeStruct((n, LANES), f32)]
        scratch = [pltpu.VMEM((8, LANES), f32)]
    return pl.pallas_call(
        functools.partial(_kv_kernel, tm=tm, pack=pack),
        grid=(n // tm,),
        in_specs=in_specs,
        out_specs=out_specs,
        out_shape=out_shape,
        scratch_shapes=scratch,
        compiler_params=_params(1),
        name="kv",
    )(*args)


def _qproj_kernel(*refs, pack):
    if pack:
        x_ref, g_ref, wq_ref, c_ref, pq_ref, cqc_ref, hs_ref, o_ref, n2_ref = refs
    else:
        x_ref, g_ref, wq_ref, o_ref = refs
    h = _rms(x_ref[...], g_ref[...]).astype(bf16)
    q = _dot(h, wq_ref[...]) * Q_SCALE
    if not pack:
        o_ref[...] = q
        return
    hi, mid, lo = _split3(c_ref[...])
    extras = _dot(hi, pq_ref[0]) + _dot(mid, pq_ref[1]) + _dot(lo, pq_ref[2]) + cqc_ref[...]
    _pack_heads(q, lambda h: extras, o_ref)
    n2_ref[...] = _head_norm2(q, hs_ref[...])


def _qproj(x, g, w_q, layer, c2=None, pq=None, cq_const=None, head_sum=None):
    n = x.shape[0]
    tm = min(ROW_TILE, n)
    pack = c2 is not None
    row = pl.BlockSpec((tm, D_MODEL), lambda i: (i, 0))
    in_specs = [row, _resident((1, D_MODEL)), _layer((D_MODEL, D_MODEL), layer)]
    args = [x, g, w_q]
    if pack:
        lanes = pl.BlockSpec((tm, LANES), lambda i: (i, 0))
        in_specs += [lanes, _resident((3, LANES, LANES)), _resident((1, LANES)), _resident((D_MODEL, LANES))]
        args += [c2, pq, cq_const, head_sum]
        out_spec = [pl.BlockSpec((N_HEADS, tm, LANES), lambda i: (0, i, 0)), lanes]
        out_shape = [jax.ShapeDtypeStruct((N_HEADS, n, LANES), bf16), jax.ShapeDtypeStruct((n, LANES), f32)]
    else:
        out_spec = row
        out_shape = jax.ShapeDtypeStruct((n, D_MODEL), f32)
    return pl.pallas_call(
        functools.partial(_qproj_kernel, pack=pack),
        grid=(n // tm,),
        in_specs=in_specs,
        out_specs=out_spec,
        out_shape=out_shape,
        compiler_params=_params(1),
        name="qproj",
    )(*args)


def _first_needed_tile(qn2, kn2, c2, t):
    n = qn2.shape[0]
    tq = 2 * t
    nq, nk = n // tq, n // t
    qn, kn, c = jnp.sqrt(qn2[:, :N_HEADS]), jnp.sqrt(kn2[:, :N_HEADS]), c2[:, :N_HEADS]
    q_max = qn.reshape(nq, tq, N_HEADS).max(axis=1)
    k_max = kn.reshape(nk, t, N_HEADS).max(axis=1)
    cq_max = c.reshape(nq, tq, N_HEADS).max(axis=1)
    ck_min = c.reshape(nk, t, N_HEADS).min(axis=1)
    m_low = -q_max * k_max.reshape(nq, 2, N_HEADS).max(axis=1)
    upper = q_max[:, None, :] * k_max[None, :, :] + cq_max[:, None, :] - ck_min[None, :, :]
    tile = jnp.arange(nk)
    dead = (upper <= m_low[:, None, :] - SKIP_MARGIN) & (tile[None, :, None] < 2 * jnp.arange(nq)[:, None, None])
    live = jnp.logical_not(dead)
    live_upto = jnp.any(live[:, None, :, :] & (tile[None, :] <= tile[:, None])[None, :, :, None], axis=2)
    lead = jnp.sum(jnp.logical_not(live_upto).astype(jnp.int32), axis=1)
    first = lead.reshape(nq, N_HEADS // 2, 2).min(axis=2)
    return ((first // 2) * 2).T.astype(jnp.int32)


def _flash_kernel(first_ref, q_ref, k_ref, vt_ref, o_ref, m_scr, acc_scr, s_even, s_odd, *, t):
    tq = 2 * t
    i = pl.program_id(1)
    first = first_ref[pl.program_id(0), i]
    for hh in range(2):
        m_scr[hh] = jnp.full((1, tq), NEG_BIG, f32)
        acc_scr[hh] = jnp.zeros((VT_ROWS, tq), f32)

    def logits(j, hh, c0):
        start = pl.multiple_of(j * t, t)
        q = q_ref[hh, c0:c0 + QUERY_CHUNK, :]
        return _dot_nt(k_ref[hh, pl.ds(start, t), :], q)

    def step(j, j_next, s_cur, s_next, diag_offset=None):
        for hh in range(2):
            for c0 in range(0, tq, QUERY_CHUNK):
                cols = slice(c0, c0 + QUERY_CHUNK)
                s = s_cur[hh, :, cols]
                s_next[hh, :, cols] = logits(j_next, hh, c0)
                if diag_offset is not None:
                    key = lax.broadcasted_iota(jnp.int32, (t, QUERY_CHUNK), 0) + (diag_offset - c0)
                    qry = lax.broadcasted_iota(jnp.int32, (t, QUERY_CHUNK), 1)
                    s = jnp.where(key <= qry, s, NEG_BIG)
                m_prev = m_scr[hh, :, cols]
                m_new = jnp.maximum(m_prev, jnp.max(s, axis=0, keepdims=True))
                p = jnp.exp2(s - m_new)
                pv = _dot(vt_ref[hh, j], p.astype(bf16))
                acc_scr[hh, :, cols] = jnp.exp2(m_prev - m_new) * acc_scr[hh, :, cols] + pv
                m_scr[hh, :, cols] = m_new

    for hh in range(2):
        for c0 in range(0, tq, QUERY_CHUNK):
            s_even[hh, :, c0:c0 + QUERY_CHUNK] = logits(2 * i, hh, c0)
    step(2 * i, 2 * i + 1, s_even, s_odd, diag_offset=0)
    step(2 * i + 1, jnp.maximum(2 * i - 1, 0), s_odd, s_even, diag_offset=t)

    def pair(n, carry):
        j = 2 * i - 1 - 2 * n
        step(j, j - 1, s_even, s_odd)
        step(j - 1, jnp.maximum(j - 2, 0), s_odd, s_even)
        return carry

    lax.fori_loop(0, (2 * i - first) // 2, pair, 0)

    outs = []
    for hh in range(2):
        acc = acc_scr[hh]
        outs.append(acc[0:HEAD_DIM, :] / acc[HEAD_DIM:HEAD_DIM + 1, :])
    o_ref[...] = jnp.transpose(jnp.concatenate(outs, axis=0)).astype(bf16)


def _flash(qp, kp, vt, first):
    n = qp.shape[1]
    t = ATT_TILE
    tq = 2 * t
    assert vt.shape == (N_HEADS, n // t, VT_ROWS, t) and n % tq == 0
    assert first.shape == (N_HEADS // 2, n // tq)
    grid_spec = pltpu.PrefetchScalarGridSpec(
        num_scalar_prefetch=1,
        grid=(N_HEADS // 2, n // tq),
        in_specs=[pl.BlockSpec((2, tq, LANES), lambda p, i, first_ref: (p, i, 0)),
                  pl.BlockSpec((2, n, LANES), lambda p, i, first_ref: (p, 0, 0)),
                  pl.BlockSpec((2, n // t, VT_ROWS, t), lambda p, i, first_ref: (p, 0, 0, 0))],
        out_specs=pl.BlockSpec((tq, LANES), lambda p, i, first_ref: (i, p)),
        scratch_shapes=[pltpu.VMEM((2, 1, tq), f32), pltpu.VMEM((2, VT_ROWS, tq), f32),
                        pltpu.VMEM((2, t, tq), f32), pltpu.VMEM((2, t, tq), f32)])
    return pl.pallas_call(
        functools.partial(_flash_kernel, t=t),
        grid_spec=grid_spec,
        out_shape=jax.ShapeDtypeStruct((n, D_MODEL), bf16),
        compiler_params=_params(2),
        name="flash",
    )(first, qp, kp, vt)


def _cum_kernel(past_ref, new_ref, o_ref, *, past_len, new_len, chunk):
    t = _tri(chunk)
    carry = jnp.zeros((1, N_HEADS), f32)
    for s in range(0, past_len, chunk):
        c = _prefix_sum(t, past_ref[0, s:s + chunk, :]) + carry
        o_ref[0, s:s + chunk, :] = c * LOG2E
        carry = c[chunk - 1:chunk, :]
    c = _prefix_sum(_tri(new_len), new_ref[0]) + carry
    o_ref[0, past_len:past_len + new_len, :] = c * LOG2E


def _cum(past_logf, new_logf):
    b, past_len, _ = past_logf.shape
    new_len = new_logf.shape[1]
    total = past_len + new_len
    return pl.pallas_call(
        functools.partial(_cum_kernel, past_len=past_len, new_len=new_len, chunk=512),
        grid=(b,),
        in_specs=[pl.BlockSpec((1, past_len, N_HEADS), lambda i: (i, 0, 0)),
                  pl.BlockSpec((1, new_len, N_HEADS), lambda i: (i, 0, 0))],
        out_specs=pl.BlockSpec((1, total, N_HEADS), lambda i: (i, 0, 0)),
        out_shape=jax.ShapeDtypeStruct((b, total, N_HEADS), f32),
        compiler_params=_params(1),
        name="cum",
    )(past_logf, new_logf)


def _expand_heads(e, x):
    hi, mid, lo = _split3(x)
    return _dot_nt(e, hi) + _dot_nt(e, mid) + _dot_nt(e, lo)


def _decode_attn_kernel(q_ref, kc_ref, vc_ref, kn_ref, vn_ref, c_ref, e_ref, o_ref, *, past_len, new_len):
    rows = N_HEADS * new_len
    q = q_ref[0]
    row_head = lax.broadcasted_iota(jnp.int32, (rows, D_MODEL), 0) // new_len
    col_head = lax.broadcasted_iota(jnp.int32, (rows, D_MODEL), 1) // HEAD_DIM
    qbd = jnp.where(row_head == col_head, jnp.tile(q, (N_HEADS, 1)), 0.0).astype(bf16)
    e = e_ref[...]
    c2 = c_ref[0]
    ck_new = _expand_heads(e, c2[past_len:past_len + new_len, :])
    t_of_row = lax.broadcasted_iota(jnp.int32, (rows, new_len), 0) % new_len
    s_idx = lax.broadcasted_iota(jnp.int32, (rows, new_len), 1)
    cq = jnp.sum(jnp.where(s_idx == t_of_row, ck_new, 0.0), axis=1, keepdims=True)

    lg = _dot_nt(qbd, kc_ref[0]) + cq - _expand_heads(e, c2[0:past_len, :])
    lg_new = jnp.where(s_idx <= t_of_row, _dot_nt(qbd, kn_ref[0].astype(bf16)) + cq - ck_new, NEG_BIG)
    m = jnp.maximum(jnp.max(lg, axis=1, keepdims=True), jnp.max(lg_new, axis=1, keepdims=True))
    p = jnp.exp2(lg - m)
    p_new = jnp.exp2(lg_new - m)
    denom = jnp.sum(p, axis=1, keepdims=True) + jnp.sum(p_new, axis=1, keepdims=True)
    o = (_dot(p.astype(bf16), vc_ref[0])
         + _dot(p_new.astype(bf16), vn_ref[0].astype(bf16))) / denom
    lane = lax.broadcasted_iota(jnp.int32, (new_len, LANES), 1)
    for pp in range(N_HEADS // 2):
        even = o[(2 * pp) * new_len:(2 * pp + 1) * new_len, pp * LANES:(pp + 1) * LANES]
        odd = o[(2 * pp + 1) * new_len:(2 * pp + 2) * new_len, pp * LANES:(pp + 1) * LANES]
        o_ref[0, :, pp * LANES:(pp + 1) * LANES] = jnp.where(lane < HEAD_DIM, even, odd).astype(bf16)


def _decode_attn(q, cache_k, cache_v, k_new, v_new, c2, expand):
    b, past_len, _ = cache_k.shape
    new_len = q.shape[1]
    cache = pl.BlockSpec((1, past_len, D_MODEL), lambda i: (i, 0, 0))
    new = pl.BlockSpec((1, new_len, D_MODEL), lambda i: (i, 0, 0))
    return pl.pallas_call(
        functools.partial(_decode_attn_kernel, past_len=past_len, new_len=new_len),
        grid=(b,),
        in_specs=[new, cache, cache, new, new,
                  pl.BlockSpec((1, past_len + new_len, N_HEADS), lambda i: (i, 0, 0)),
                  _resident((N_HEADS * new_len, N_HEADS))],
        out_specs=new,
        out_shape=jax.ShapeDtypeStruct((b, new_len, D_MODEL), bf16),
        compiler_params=_params(1),
        name="decode_attn",
    )(q, cache_k, cache_v, k_new, v_new, c2, expand)


def kernel(x_prompt, x_sample, cache_k, cache_v, cache_logf, state_conv, norm_mix, norm_mlp, conv_in, conv_w,
           conv_out, norm_kv, w_k, w_v, w_f, b_f, w_q, w_o, mlp_up, mlp_down, norm_final):
    assert x_prompt.shape[0] == 1 and x_prompt.shape[2] == D_MODEL
    seq = x_prompt.shape[1]
    dec_b, dec_s, _ = x_sample.shape
    past_len = cache_k.shape[1]
    assert seq % ROW_TILE == 0 and ROW_TILE == ATT_TILE and (dec_b * dec_s) % 8 == 0

    pk, ck_const, pq, cq_const, expand, head_sum = _placement_constants()
    row = lambda a: a.reshape(1, -1)
    conv_in_b, conv_out_b = conv_in.astype(bf16), conv_out.astype(bf16)
    w_k_b, w_v_b, w_q_b, w_o_b = w_k.astype(bf16), w_v.astype(bf16), w_q.astype(bf16), w_o.astype(bf16)
    up_b, down_b = mlp_up.astype(bf16), mlp_down.astype(bf16)
    w_f_b = jnp.pad(w_f, ((0, 0), (0, LANES - N_HEADS))).astype(bf16)
    b_f_p = jnp.pad(b_f, (0, LANES - N_HEADS)).reshape(1, LANES)

    def mlp(x, l, **kw):
        return _mlp(x, row(norm_mlp[l]), up_b, down_b, l, **kw)

    x = x_prompt.reshape(seq, D_MODEL)
    tails = []
    for l in range(N_A):
        x, tail = _mixer(x, row(norm_mix[l]), conv_in_b, conv_w, conv_out_b, l)
        tails.append(tail[6:8])
        x = mlp(x, l)
    k_p, v_p, logf_p, c2_p, kp, vt, kn2 = _kv(x, row(norm_kv), w_k_b, w_v_b, w_f_b, b_f_p, pk, ck_const, head_sum)
    for l in range(N_A, DEPTH):
        j = l - N_A
        qp, qn2 = _qproj(x, row(norm_mix[l]), w_q_b, j, c2_p, pq, cq_const, head_sum)
        attn = _flash(qp, kp, vt, _first_needed_tile(qn2, kn2, c2_p, ATT_TILE))
        x = mlp(x, l, attn=attn, w_o=w_o_b, attn_layer=j,
                g_final=row(norm_final) if l == DEPTH - 1 else None)
    y_prompt = x.reshape(1, seq, D_MODEL)
    conv_prompt = jnp.stack(tails).reshape(N_A, 1, 2, D_MODEL)

    n_s = dec_b * dec_s
    x = x_sample.reshape(n_s, D_MODEL)
    new_states = []
    for l in range(N_A):
        st = state_conv[l]
        pad = jnp.zeros((dec_b, dec_s - 2, D_MODEL), f32)
        s2 = jnp.concatenate([st, pad], axis=1).reshape(n_s, D_MODEL)
        s1 = jnp.concatenate([st[:, 1:2], pad, pad[:, 0:1]], axis=1).reshape(n_s, D_MODEL)
        x, u = _mixer(x, row(norm_mix[l]), conv_in_b, conv_w, conv_out_b, l,
                      state_rows=(s1, s2), seq_len=dec_s)
        new_states.append(u.reshape(dec_b, dec_s, D_MODEL)[:, dec_s - 2:])
        x = mlp(x, l)
    k_s, v_s, logf_s = _kv(x, row(norm_kv), w_k_b, w_v_b, w_f_b, b_f_p)
    c2_s = _cum(cache_logf, logf_s.reshape(dec_b, dec_s, N_HEADS))
    k_s3, v_s3 = k_s.reshape(dec_b, dec_s, D_MODEL), v_s.reshape(dec_b, dec_s, D_MODEL)
    cache_k3 = cache_k.astype(bf16).reshape(dec_b, past_len, D_MODEL)
    cache_v3 = cache_v.astype(bf16).reshape(dec_b, past_len, D_MODEL)
    for l in range(N_A, DEPTH):
        j = l - N_A
        q = _qproj(x, row(norm_mix[l]), w_q_b, j)
        attn = _decode_attn(q.reshape(dec_b, dec_s, D_MODEL), cache_k3, cache_v3, k_s3, v_s3, c2_s, expand)
        x = mlp(x, l, attn=attn.reshape(n_s, D_MODEL), w_o=w_o_b, attn_layer=j,
                g_final=row(norm_final) if l == DEPTH - 1 else None)
    y_sample = x.reshape(dec_b, dec_s, D_MODEL)

    return (y_prompt, y_sample,
            k_p.reshape(1, seq, N_HEADS, HEAD_DIM), v_p.reshape(1, seq, N_HEADS, HEAD_DIM),
            logf_p.reshape(1, seq, N_HEADS), conv_prompt,
            k_s.reshape(dec_b, dec_s, N_HEADS, HEAD_DIM), v_s.reshape(dec_b, dec_s, N_HEADS, HEAD_DIM),
            logf_s.reshape(dec_b, dec_s, N_HEADS), jnp.stack(new_states))
```

```python
import functools
import math

import numpy as np
import jax
import jax.numpy as jnp
from jax import lax
from jax.experimental import pallas as pl
from jax.experimental.pallas import tpu as pltpu

D_MODEL = 1024
N_HEADS = 16
HEAD_DIM = 64
D_FF = 4 * D_MODEL
N_A = 2
DEPTH = 4
RMS_EPS = 1e-6
LOG2E = math.log2(math.e)
Q_SCALE = HEAD_DIM ** -0.5 * LOG2E
NEG_BIG = -1e30

LANES = 128
ROW_TILE = 512
ATT_TILE = 512
QUERY_CHUNK = 256
SKIP_MARGIN = 170.0
NORM_SLACK = 1.01
VMEM_LIMIT = 56 * 1024 * 1024

CQ_LANE0 = 64
CK_LANE0 = 112
VT_ROWS = 80

bf16 = jnp.bfloat16
f32 = jnp.float32


def _dot(a, b):
    return jnp.dot(a, b, preferred_element_type=f32)


def _dot_nt(a, b):
    return lax.dot_general(a, b, (((1,), (1,)), ((), ())), preferred_element_type=f32)


def _rms(x, g):
    ms = jnp.mean(x * x, axis=-1, keepdims=True)
    return x * lax.rsqrt(ms + RMS_EPS) * g


def _split3(x):
    hi = x.astype(bf16)
    r1 = x - hi.astype(f32)
    mid = r1.astype(bf16)
    lo = (r1 - mid.astype(f32)).astype(bf16)
    return hi, mid, lo


def _tri(n):
    r = lax.broadcasted_iota(jnp.int32, (n, n), 0)
    c = lax.broadcasted_iota(jnp.int32, (n, n), 1)
    return jnp.where(c <= r, 1.0, 0.0).astype(bf16)


def _prefix_sum(t, x):
    hi, mid, lo = _split3(x)
    return _dot(t, hi) + _dot(t, mid) + _dot(t, lo)


def _log_sigmoid(z):
    return jnp.minimum(z, 0.0) - jnp.log1p(jnp.exp(-jnp.abs(z)))


def _resident(shape):
    nd = len(shape)
    return pl.BlockSpec(shape, lambda *_: (0,) * nd, pipeline_mode=pl.Buffered(1))


def _layer(shape, layer):
    nd = len(shape)
    return pl.BlockSpec((None,) + tuple(shape), lambda *_: (layer,) + (0,) * nd, pipeline_mode=pl.Buffered(1))


def _params(n_axes):
    return pltpu.CompilerParams(dimension_semantics=("arbitrary",) * n_axes,
                                vmem_limit_bytes=VMEM_LIMIT)


def _placement_constants():
    pk = np.zeros((3, LANES, N_HEADS * LANES), np.float32)
    ck_const = np.zeros((1, N_HEADS * LANES), np.float32)
    pq = np.zeros((3, LANES, LANES), np.float32)
    cq_const = np.zeros((1, LANES), np.float32)
    for j in range(3):
        cq_const[0, CK_LANE0 + j] = 1.0
        for h in range(N_HEADS):
            pk[j, h, h * LANES + CK_LANE0 + j] = -1.0
            ck_const[0, h * LANES + CQ_LANE0 + 16 * j + h] = 1.0
            pq[j, h, CQ_LANE0 + 16 * j + h] = 1.0
    expand = np.zeros((N_HEADS * 16, N_HEADS), np.float32)
    for h in range(N_HEADS):
        expand[h * 16:(h + 1) * 16, h] = 1.0
    head_sum = np.zeros((D_MODEL, LANES), np.float32)
    for h in range(N_HEADS):
        head_sum[h * HEAD_DIM:(h + 1) * HEAD_DIM, h] = 1.0
    return (jnp.asarray(pk, bf16), jnp.asarray(ck_const), jnp.asarray(pq, bf16),
            jnp.asarray(cq_const), jnp.asarray(expand, bf16), jnp.asarray(head_sum, bf16))


def _head_norm2(x, head_sum):
    xb = x.astype(bf16).astype(f32)
    return _dot((xb * xb).astype(bf16), head_sum) * NORM_SLACK


def _pack_heads(x, extras_of_head, out_ref):
    lane = lax.broadcasted_iota(jnp.int32, (x.shape[0], LANES), 1)
    for h in range(N_HEADS):
        pair = x[:, (h // 2) * LANES:(h // 2 + 1) * LANES]
        base = pair if h % 2 == 0 else pltpu.roll(pair, HEAD_DIM, axis=1)
        out_ref[h] = jnp.where(lane < HEAD_DIM, base, extras_of_head(h)).astype(bf16)


def _mixer_kernel(*refs, tm, carry_rows, seq_len):
    if carry_rows:
        x_ref, g_ref, win_ref, cw_ref, wout_ref, o_ref, u_ref, u_scr = refs
    else:
        x_ref, g_ref, win_ref, cw_ref, wout_ref, s1_ref, s2_ref, o_ref, u_ref, u_scr = refs

    @pl.when(pl.program_id(0) == 0)
    def _():
        u_scr[0:8, :] = jnp.zeros((8, D_MODEL), f32)

    x = x_ref[...]
    h = _rms(x, g_ref[...]).astype(bf16)
    bcx = _dot(h, win_ref[...])
    gb = bcx[:, 0:D_MODEL]
    u = bcx[:, D_MODEL:2 * D_MODEL] * bcx[:, 2 * D_MODEL:3 * D_MODEL]
    u_scr[8:8 + tm, :] = u
    u1 = u_scr[7:7 + tm, :]
    u2 = u_scr[6:6 + tm, :]
    if carry_rows:
        tail = u_scr[tm:tm + 8, :]
        u_scr[0:8, :] = tail
        u_ref[...] = tail
    else:
        pos = lax.broadcasted_iota(jnp.int32, (tm, 1), 0) % seq_len
        u1 = jnp.where(pos < 1, s1_ref[...], u1)
        u2 = jnp.where(pos < 2, s2_ref[...], u2)
        u_ref[...] = u
    cw = cw_ref[...]
    conv = cw[0:1, :] * u2 + cw[1:2, :] * u1 + cw[2:3, :] * u
    o_ref[...] = x + _dot((gb * conv).astype(bf16), wout_ref[...])


def _mixer(x, g, w_in, cw, w_out, layer, state_rows=None, seq_len=None):
    n = x.shape[0]
    tm = min(ROW_TILE, n)
    carry = state_rows is None
    row = pl.BlockSpec((tm, D_MODEL), lambda i: (i, 0))
    in_specs = [row, _resident((1, D_MODEL)), _layer((D_MODEL, 3 * D_MODEL), layer),
                _layer((3, D_MODEL), layer), _layer((D_MODEL, D_MODEL), layer)]
    args = [x, g, w_in, cw, w_out]
    if carry:
        u_spec = pl.BlockSpec((8, D_MODEL), lambda i: (0, 0))
        u_shape = jax.ShapeDtypeStruct((8, D_MODEL), f32)
    else:
        assert n == tm
        in_specs += [row, row]
        args += list(state_rows)
        u_spec = row
        u_shape = jax.ShapeDtypeStruct((n, D_MODEL), f32)
    return pl.pallas_call(
        functools.partial(_mixer_kernel, tm=tm, carry_rows=carry, seq_len=seq_len),
        grid=(n // tm,),
        in_specs=in_specs,
        out_specs=[row, u_spec],
        out_shape=[jax.ShapeDtypeStruct((n, D_MODEL), f32), u_shape],
        scratch_shapes=[pltpu.VMEM((tm + 8, D_MODEL), f32)],
        compiler_params=_params(1),
        name="mixer",
    )(*args)


def _mlp_kernel(*refs, has_attn, final_norm):
    refs = list(refs)
    x_ref = refs.pop(0)
    if has_attn:
        attn_ref = refs.pop(0)
        wo_ref = refs.pop(0)
    g_ref, wup_ref, wdn_ref = refs[0:3]
    gf_ref = refs[3] if final_norm else None
    o_ref = refs[-1]

    x = x_ref[...]
    if has_attn:
        x = x + _dot(attn_ref[...], wo_ref[...])
    h = _rms(x, g_ref[...]).astype(bf16)
    a = jnp.maximum(_dot(h, wup_ref[...]), 0.0)
    y = x + _dot((a * a).astype(bf16), wdn_ref[...])
    if final_norm:
        y = _rms(y, gf_ref[...])
    o_ref[...] = y


def _mlp(x, g, w_up, w_down, layer, attn=None, w_o=None, attn_layer=None, g_final=None):
    n = x.shape[0]
    tm = min(ROW_TILE, n)
    row = pl.BlockSpec((tm, D_MODEL), lambda i: (i, 0))
    in_specs, args = [row], [x]
    if attn is not None:
        in_specs += [row, _layer((D_MODEL, D_MODEL), attn_layer)]
        args += [attn, w_o]
    in_specs += [_resident((1, D_MODEL)), _layer((D_MODEL, D_FF), layer), _layer((D_FF, D_MODEL), layer)]
    args += [g, w_up, w_down]
    if g_final is not None:
        in_specs.append(_resident((1, D_MODEL)))
        args.append(g_final)
    return pl.pallas_call(
        functools.partial(_mlp_kernel, has_attn=attn is not None, final_norm=g_final is not None),
        grid=(n // tm,),
        in_specs=in_specs,
        out_specs=row,
        out_shape=jax.ShapeDtypeStruct((n, D_MODEL), f32),
        compiler_params=_params(1),
        name="mlp",
    )(*args)


def _kv_kernel(*refs, tm, pack):
    if pack:
        (x_ref, g_ref, wk_ref, wv_ref, wf_ref, bf_ref, pk_ref, ckc_ref, hs_ref,
         k_ref, v_ref, lf_ref, c_ref, kp_ref, vt_ref, n2_ref, carry) = refs
    else:
        x_ref, g_ref, wk_ref, wv_ref, wf_ref, bf_ref, k_ref, v_ref, lf_ref = refs

    hk = _rms(x_ref[...], g_ref[...]).astype(bf16)
    k = _dot(hk, wk_ref[...])
    v = _dot(hk, wv_ref[...])
    z = _dot(hk, wf_ref[...]) + bf_ref[...]
    lane = lax.broadcasted_iota(jnp.int32, (tm, LANES), 1)
    logf = jnp.where(lane < N_HEADS, _log_sigmoid(z), 0.0)
    k_ref[...] = k
    v_ref[...] = v
    lf_ref[...] = logf[:, 0:N_HEADS]
    if not pack:
        return

    @pl.when(pl.program_id(0) == 0)
    def _():
        carry[...] = jnp.zeros_like(carry)

    c = _prefix_sum(_tri(tm), logf) + carry[0:1, :]
    carry[0:1, :] = c[tm - 1:tm, :]
    c2 = c * LOG2E
    c_ref[...] = c2
    hi, mid, lo = _split3(c2)
    extras = _dot(hi, pk_ref[0]) + _dot(mid, pk_ref[1]) + _dot(lo, pk_ref[2]) + ckc_ref[...]
    _pack_heads(k, lambda h: extras[:, h * LANES:(h + 1) * LANES], kp_ref)
    n2_ref[...] = _head_norm2(k, hs_ref[...])
    pad_row = lax.broadcasted_iota(jnp.int32, (VT_ROWS - HEAD_DIM, tm), 0)
    ones_row = jnp.where(pad_row == 0, 1.0, 0.0)
    for p in range(N_HEADS // 2):
        vt = jnp.transpose(v[:, p * LANES:(p + 1) * LANES])
        for hh in range(2):
            rows = vt[hh * HEAD_DIM:(hh + 1) * HEAD_DIM, :]
            vt_ref[2 * p + hh, 0] = jnp.concatenate([rows, ones_row], axis=0).astype(bf16)


def _kv(x, g, w_k, w_v, w_f, b_f, pk=None, ck_const=None, head_sum=None):
    n = x.shape[0]
    tm = min(ROW_TILE, n)
    pack = pk is not None
    row = pl.BlockSpec((tm, D_MODEL), lambda i: (i, 0))
    heads = pl.BlockSpec((N_HEADS, tm, LANES), lambda i: (0, i, 0))
    in_specs = [row, _resident((1, D_MODEL)), _resident((D_MODEL, D_MODEL)), _resident((D_MODEL, D_MODEL)),
                _resident((D_MODEL, LANES)), _resident((1, LANES))]
    args = [x, g, w_k, w_v, w_f, b_f]
    out_specs = [row, row, pl.BlockSpec((tm, N_HEADS), lambda i: (i, 0))]
    out_shape = [jax.ShapeDtypeStruct((n, D_MODEL), f32), jax.ShapeDtypeStruct((n, D_MODEL), f32),
                 jax.ShapeDtypeStruct((n, N_HEADS), f32)]
    scratch = []
    if pack:
        lanes = pl.BlockSpec((tm, LANES), lambda i: (i, 0))
        in_specs += [_resident((3, LANES, N_HEADS * LANES)), _resident((1, N_HEADS * LANES)),
                     _resident((D_MODEL, LANES))]
        args += [pk, ck_const, head_sum]
        out_specs += [lanes, heads, pl.BlockSpec((N_HEADS, 1, VT_ROWS, tm), lambda i: (0, i, 0, 0)), lanes]
        out_shape += [jax.ShapeDtypeStruct((n, LANES), f32),
                      jax.ShapeDtypeStruct((N_HEADS, n, LANES), bf16),
                      jax.ShapeDtypeStruct((N_HEADS, n // tm, VT_ROWS, tm), bf16),
                      jax.ShapeDtypeStruct((n, LANES), f32)]
        scratch = [pltpu.VMEM((8, LANES), f32)]
    return pl.pallas_call(
        functools.partial(_kv_kernel, tm=tm, pack=pack),
        grid=(n // tm,),
        in_specs=in_specs,
        out_specs=out_specs,
        out_shape=out_shape,
        scratch_shapes=scratch,
        compiler_params=_params(1),
        name="kv",
    )(*args)


def _qproj_kernel(*refs, pack):
    if pack:
        x_ref, g_ref, wq_ref, c_ref, pq_ref, cqc_ref, hs_ref, o_ref, n2_ref = refs
    else:
        x_ref, g_ref, wq_ref, o_ref = refs
    h = _rms(x_ref[...], g_ref[...]).astype(bf16)
    q = _dot(h, wq_ref[...]) * Q_SCALE
    if not pack:
        o_ref[...] = q
        return
    hi, mid, lo = _split3(c_ref[...])
    extras = _dot(hi, pq_ref[0]) + _dot(mid, pq_ref[1]) + _dot(lo, pq_ref[2]) + cqc_ref[...]
    _pack_heads(q, lambda h: extras, o_ref)
    n2_ref[...] = _head_norm2(q, hs_ref[...])


def _qproj(x, g, w_q, layer, c2=None, pq=None, cq_const=None, head_sum=None):
    n = x.shape[0]
    tm = min(ROW_TILE, n)
    pack = c2 is not None
    row = pl.BlockSpec((tm, D_MODEL), lambda i: (i, 0))
    in_specs = [row, _resident((1, D_MODEL)), _layer((D_MODEL, D_MODEL), layer)]
    args = [x, g, w_q]
    if pack:
        lanes = pl.BlockSpec((tm, LANES), lambda i: (i, 0))
        in_specs += [lanes, _resident((3, LANES, LANES)), _resident((1, LANES)), _resident((D_MODEL, LANES))]
        args += [c2, pq, cq_const, head_sum]
        out_spec = [pl.BlockSpec((N_HEADS, tm, LANES), lambda i: (0, i, 0)), lanes]
        out_shape = [jax.ShapeDtypeStruct((N_HEADS, n, LANES), bf16), jax.ShapeDtypeStruct((n, LANES), f32)]
    else:
        out_spec = row
        out_shape = jax.ShapeDtypeStruct((n, D_MODEL), f32)
    return pl.pallas_call(
        functools.partial(_qproj_kernel, pack=pack),
        grid=(n // tm,),
        in_specs=in_specs,
        out_specs=out_spec,
        out_shape=out_shape,
        compiler_params=_params(1),
        name="qproj",
    )(*args)


def _first_needed_tile(qn2, kn2, c2, t):
    n = qn2.shape[0]
    tq = 2 * t
    nq, nk = n // tq, n // t
    qn, kn, c = jnp.sqrt(qn2[:, :N_HEADS]), jnp.sqrt(kn2[:, :N_HEADS]), c2[:, :N_HEADS]
    q_max = qn.reshape(nq, tq, N_HEADS).max(axis=1)
    k_max = kn.reshape(nk, t, N_HEADS).max(axis=1)
    cq_max = c.reshape(nq, tq, N_HEADS).max(axis=1)
    ck_min = c.reshape(nk, t, N_HEADS).min(axis=1)
    m_low = -q_max * k_max.reshape(nq, 2, N_HEADS).max(axis=1)
    upper = q_max[:, None, :] * k_max[None, :, :] + cq_max[:, None, :] - ck_min[None, :, :]
    tile = jnp.arange(nk)
    dead = (upper <= m_low[:, None, :] - SKIP_MARGIN) & (tile[None, :, None] < 2 * jnp.arange(nq)[:, None, None])
    live = jnp.logical_not(dead)
    live_upto = jnp.any(live[:, None, :, :] & (tile[None, :] <= tile[:, None])[None, :, :, None], axis=2)
    lead = jnp.sum(jnp.logical_not(live_upto).astype(jnp.int32), axis=1)
    return ((lead // 2) * 2).T.astype(jnp.int32)


def _flash_kernel(first_ref, q_ref, k_ref, vt_ref, o_ref, m_scr, acc_scr, s_even, s_odd, *, t):
    tq = 2 * t
    i = pl.program_id(1)
    first_a = first_ref[2 * pl.program_id(0), i]
    first_b = first_ref[2 * pl.program_id(0) + 1, i]
    both_until = jnp.maximum(first_a, first_b)
    alone_until = jnp.minimum(first_a, first_b)
    alone_head = jnp.where(first_a < first_b, 0, 1)
    for hh in range(2):
        m_scr[hh] = jnp.full((1, tq), NEG_BIG, f32)
        acc_scr[hh] = jnp.zeros((VT_ROWS, tq), f32)

    def logits(j, hh, c0):
        start = pl.multiple_of(j * t, t)
        q = q_ref[hh, c0:c0 + QUERY_CHUNK, :]
        return _dot_nt(k_ref[hh, pl.ds(start, t), :], q)

    def step(j, j_next, s_cur, s_next, heads=(0, 1), diag_offset=None):
        for hh in heads:
            for c0 in range(0, tq, QUERY_CHUNK):
                cols = slice(c0, c0 + QUERY_CHUNK)
                s = s_cur[hh, :, cols]
                s_next[hh, :, cols] = logits(j_next, hh, c0)
                if diag_offset is not None:
                    key = lax.broadcasted_iota(jnp.int32, (t, QUERY_CHUNK), 0) + (diag_offset - c0)
                    qry = lax.broadcasted_iota(jnp.int32, (t, QUERY_CHUNK), 1)
                    s = jnp.where(key <= qry, s, NEG_BIG)
                m_prev = m_scr[hh, :, cols]
                m_new = jnp.maximum(m_prev, jnp.max(s, axis=0, keepdims=True))
                p = jnp.exp2(s - m_new)
                pv = _dot(vt_ref[hh, j], p.astype(bf16))
                acc_scr[hh, :, cols] = jnp.exp2(m_prev - m_new) * acc_scr[hh, :, cols] + pv
                m_scr[hh, :, cols] = m_new

    for hh in range(2):
        for c0 in range(0, tq, QUERY_CHUNK):
            s_even[hh, :, c0:c0 + QUERY_CHUNK] = logits(2 * i, hh, c0)
    step(2 * i, 2 * i + 1, s_even, s_odd, diag_offset=0)
    step(2 * i + 1, jnp.maximum(2 * i - 1, 0), s_odd, s_even, diag_offset=t)

    def pair(n, carry):
        j = 2 * i - 1 - 2 * n
        step(j, j - 1, s_even, s_odd)
        step(j - 1, jnp.maximum(j - 2, 0), s_odd, s_even)
        return carry

    lax.fori_loop(0, (2 * i - both_until) // 2, pair, 0)

    def pair_alone(n, carry):
        j = both_until - 1 - 2 * n
        step(j, j - 1, s_even, s_odd, heads=(alone_head,))
        step(j - 1, jnp.maximum(j - 2, 0), s_odd, s_even, heads=(alone_head,))
        return carry

    lax.fori_loop(0, (both_until - alone_until) // 2, pair_alone, 0)

    outs = []
    for hh in range(2):
        acc = acc_scr[hh]
        outs.append(acc[0:HEAD_DIM, :] / acc[HEAD_DIM:HEAD_DIM + 1, :])
    o_ref[...] = jnp.transpose(jnp.concatenate(outs, axis=0)).astype(bf16)


def _flash(qp, kp, vt, first):
    n = qp.shape[1]
    t = ATT_TILE
    tq = 2 * t
    assert vt.shape == (N_HEADS, n // t, VT_ROWS, t) and n % tq == 0
    assert first.shape == (N_HEADS, n // tq)
    grid_spec = pltpu.PrefetchScalarGridSpec(
        num_scalar_prefetch=1,
        grid=(N_HEADS // 2, n // tq),
        in_specs=[pl.BlockSpec((2, tq, LANES), lambda p, i, first_ref: (p, i, 0)),
                  pl.BlockSpec((2, n, LANES), lambda p, i, first_ref: (p, 0, 0)),
                  pl.BlockSpec((2, n // t, VT_ROWS, t), lambda p, i, first_ref: (p, 0, 0, 0))],
        out_specs=pl.BlockSpec((tq, LANES), lambda p, i, first_ref: (i, p)),
        scratch_shapes=[pltpu.VMEM((2, 1, tq), f32), pltpu.VMEM((2, VT_ROWS, tq), f32),
                        pltpu.VMEM((2, t, tq), f32), pltpu.VMEM((2, t, tq), f32)])
    return pl.pallas_call(
        functools.partial(_flash_kernel, t=t),
        grid_spec=grid_spec,
        out_shape=jax.ShapeDtypeStruct((n, D_MODEL), bf16),
        compiler_params=_params(2),
        name="flash",
    )(first, qp, kp, vt)


def _cum_kernel(past_ref, new_ref, o_ref, *, past_len, new_len, chunk):
    t = _tri(chunk)
    carry = jnp.zeros((1, N_HEADS), f32)
    for s in range(0, past_len, chunk):
        c = _prefix_sum(t, past_ref[0, s:s + chunk, :]) + carry
        o_ref[0, s:s + chunk, :] = c * LOG2E
        carry = c[chunk - 1:chunk, :]
    c = _prefix_sum(_tri(new_len), new_ref[0]) + carry
    o_ref[0, past_len:past_len + new_len, :] = c * LOG2E


def _cum(past_logf, new_logf):
    b, past_len, _ = past_logf.shape
    new_len = new_logf.shape[1]
    total = past_len + new_len
    return pl.pallas_call(
        functools.partial(_cum_kernel, past_len=past_len, new_len=new_len, chunk=512),
        grid=(b,),
        in_specs=[pl.BlockSpec((1, past_len, N_HEADS), lambda i: (i, 0, 0)),
                  pl.BlockSpec((1, new_len, N_HEADS), lambda i: (i, 0, 0))],
        out_specs=pl.BlockSpec((1, total, N_HEADS), lambda i: (i, 0, 0)),
        out_shape=jax.ShapeDtypeStruct((b, total, N_HEADS), f32),
        compiler_params=_params(1),
        name="cum",
    )(past_logf, new_logf)


def _expand_heads(e, x):
    hi, mid, lo = _split3(x)
    return _dot_nt(e, hi) + _dot_nt(e, mid) + _dot_nt(e, lo)


def _decode_attn_kernel(q_ref, kc_ref, vc_ref, kn_ref, vn_ref, c_ref, e_ref, o_ref, *, past_len, new_len):
    rows = N_HEADS * new_len
    q = q_ref[0]
    row_head = lax.broadcasted_iota(jnp.int32, (rows, D_MODEL), 0) // new_len
    col_head = lax.broadcasted_iota(jnp.int32, (rows, D_MODEL), 1) // HEAD_DIM
    qbd = jnp.where(row_head == col_head, jnp.tile(q, (N_HEADS, 1)), 0.0).astype(bf16)
    e = e_ref[...]
    c2 = c_ref[0]
    ck_new = _expand_heads(e, c2[past_len:past_len + new_len, :])
    t_of_row = lax.broadcasted_iota(jnp.int32, (rows, new_len), 0) % new_len
    s_idx = lax.broadcasted_iota(jnp.int32, (rows, new_len), 1)
    cq = jnp.sum(jnp.where(s_idx == t_of_row, ck_new, 0.0), axis=1, keepdims=True)

    lg = _dot_nt(qbd, kc_ref[0]) + cq - _expand_heads(e, c2[0:past_len, :])
    lg_new = jnp.where(s_idx <= t_of_row, _dot_nt(qbd, kn_ref[0].astype(bf16)) + cq - ck_new, NEG_BIG)
    m = jnp.maximum(jnp.max(lg, axis=1, keepdims=True), jnp.max(lg_new, axis=1, keepdims=True))
    p = jnp.exp2(lg - m)
    p_new = jnp.exp2(lg_new - m)
    denom = jnp.sum(p, axis=1, keepdims=True) + jnp.sum(p_new, axis=1, keepdims=True)
    o = (_dot(p.astype(bf16), vc_ref[0])
         + _dot(p_new.astype(bf16), vn_ref[0].astype(bf16))) / denom
    lane = lax.broadcasted_iota(jnp.int32, (new_len, LANES), 1)
    for pp in range(N_HEADS // 2):
        even = o[(2 * pp) * new_len:(2 * pp + 1) * new_len, pp * LANES:(pp + 1) * LANES]
        odd = o[(2 * pp + 1) * new_len:(2 * pp + 2) * new_len, pp * LANES:(pp + 1) * LANES]
        o_ref[0, :, pp * LANES:(pp + 1) * LANES] = jnp.where(lane < HEAD_DIM, even, odd).astype(bf16)


def _decode_attn(q, cache_k, cache_v, k_new, v_new, c2, expand):
    b, past_len, _ = cache_k.shape
    new_len = q.shape[1]
    cache = pl.BlockSpec((1, past_len, D_MODEL), lambda i: (i, 0, 0))
    new = pl.BlockSpec((1, new_len, D_MODEL), lambda i: (i, 0, 0))
    return pl.pallas_call(
        functools.partial(_decode_attn_kernel, past_len=past_len, new_len=new_len),
        grid=(b,),
        in_specs=[new, cache, cache, new, new,
                  pl.BlockSpec((1, past_len + new_len, N_HEADS), lambda i: (i, 0, 0)),
                  _resident((N_HEADS * new_len, N_HEADS))],
        out_specs=new,
        out_shape=jax.ShapeDtypeStruct((b, new_len, D_MODEL), bf16),
        compiler_params=_params(1),
        name="decode_attn",
    )(q, cache_k, cache_v, k_new, v_new, c2, expand)


def kernel(x_prompt, x_sample, cache_k, cache_v, cache_logf, state_conv, norm_mix, norm_mlp, conv_in, conv_w,
           conv_out, norm_kv, w_k, w_v, w_f, b_f, w_q, w_o, mlp_up, mlp_down, norm_final):
    assert x_prompt.shape[0] == 1 and x_prompt.shape[2] == D_MODEL
    seq = x_prompt.shape[1]
    dec_b, dec_s, _ = x_sample.shape
    past_len = cache_k.shape[1]
    assert seq % ROW_TILE == 0 and ROW_TILE == ATT_TILE and (dec_b * dec_s) % 8 == 0

    pk, ck_const, pq, cq_const, expand, head_sum = _placement_constants()
    row = lambda a: a.reshape(1, -1)
    conv_in_b, conv_out_b = conv_in.astype(bf16), conv_out.astype(bf16)
    w_k_b, w_v_b, w_q_b, w_o_b = w_k.astype(bf16), w_v.astype(bf16), w_q.astype(bf16), w_o.astype(bf16)
    up_b, down_b = mlp_up.astype(bf16), mlp_down.astype(bf16)
    w_f_b = jnp.pad(w_f, ((0, 0), (0, LANES - N_HEADS))).astype(bf16)
    b_f_p = jnp.pad(b_f, (0, LANES - N_HEADS)).reshape(1, LANES)

    def mlp(x, l, **kw):
        return _mlp(x, row(norm_mlp[l]), up_b, down_b, l, **kw)

    x = x_prompt.reshape(seq, D_MODEL)
    tails = []
    for l in range(N_A):
        x, tail = _mixer(x, row(norm_mix[l]), conv_in_b, conv_w, conv_out_b, l)
        tails.append(tail[6:8])
        x = mlp(x, l)
    k_p, v_p, logf_p, c2_p, kp, vt, kn2 = _kv(x, row(norm_kv), w_k_b, w_v_b, w_f_b, b_f_p, pk, ck_const, head_sum)
    for l in range(N_A, DEPTH):
        j = l - N_A
        qp, qn2 = _qproj(x, row(norm_mix[l]), w_q_b, j, c2_p, pq, cq_const, head_sum)
        attn = _flash(qp, kp, vt, _first_needed_tile(qn2, kn2, c2_p, ATT_TILE))
        x = mlp(x, l, attn=attn, w_o=w_o_b, attn_layer=j,
                g_final=row(norm_final) if l == DEPTH - 1 else None)
    y_prompt = x.reshape(1, seq, D_MODEL)
    conv_prompt = jnp.stack(tails).reshape(N_A, 1, 2, D_MODEL)

    n_s = dec_b * dec_s
    x = x_sample.reshape(n_s, D_MODEL)
    new_states = []
    for l in range(N_A):
        st = state_conv[l]
        pad = jnp.zeros((dec_b, dec_s - 2, D_MODEL), f32)
        s2 = jnp.concatenate([st, pad], axis=1).reshape(n_s, D_MODEL)
        s1 = jnp.concatenate([st[:, 1:2], pad, pad[:, 0:1]], axis=1).reshape(n_s, D_MODEL)
        x, u = _mixer(x, row(norm_mix[l]), conv_in_b, conv_w, conv_out_b, l,
                      state_rows=(s1, s2), seq_len=dec_s)
        new_states.append(u.reshape(dec_b, dec_s, D_MODEL)[:, dec_s - 2:])
        x = mlp(x, l)
    k_s, v_s, logf_s = _kv(x, row(norm_kv), w_k_b, w_v_b, w_f_b, b_f_p)
    c2_s = _cum(cache_logf, logf_s.reshape(dec_b, dec_s, N_HEADS))
    k_s3, v_s3 = k_s.reshape(dec_b, dec_s, D_MODEL), v_s.reshape(dec_b, dec_s, D_MODEL)
    cache_k3 = cache_k.astype(bf16).reshape(dec_b, past_len, D_MODEL)
    cache_v3 = cache_v.astype(bf16).reshape(dec_b, past_len, D_MODEL)
    for l in range(N_A, DEPTH):
        j = l - N_A
        q = _qproj(x, row(norm_mix[l]), w_q_b, j)
        attn = _decode_attn(q.reshape(dec_b, dec_s, D_MODEL), cache_k3, cache_v3, k_s3, v_s3, c2_s, expand)
        x = mlp(x, l, attn=attn.reshape(n_s, D_MODEL), w_o=w_o_b, attn_layer=j,
                g_final=row(norm_final) if l == DEPTH - 1 else None)
    y_sample = x.reshape(dec_b, dec_s, D_MODEL)

    return (y_prompt, y_sample,
            k_p.reshape(1, seq, N_HEADS, HEAD_DIM), v_p.reshape(1, seq, N_HEADS, HEAD_DIM),
            logf_p.reshape(1, seq, N_HEADS), conv_prompt,
            k_s.reshape(dec_b, dec_s, N_HEADS, HEAD_DIM), v_s.reshape(dec_b, dec_s, N_HEADS, HEAD_DIM),
            logf_s.reshape(dec_b, dec_s, N_HEADS), jnp.stack(new_states))
```

```python
import functools
import math

import numpy as np
import jax
import jax.numpy as jnp
from jax import lax
from jax.experimental import pallas as pl
from jax.experimental.pallas import tpu as pltpu

D_MODEL = 1024
N_HEADS = 16
HEAD_DIM = 64
D_FF = 4 * D_MODEL
N_A = 2
DEPTH = 4
RMS_EPS = 1e-6
LOG2E = math.log2(math.e)
Q_SCALE = HEAD_DIM ** -0.5 * LOG2E
NEG_BIG = -1e30

LANES = 128
ROW_TILE = 512
ATT_TILE = 512
QUERY_CHUNK = 256
SKIP_MARGIN = 156.0
NORM_SLACK = 1.01
VMEM_LIMIT = 56 * 1024 * 1024

CQ_LANE0 = 64
CK_LANE0 = 112
VT_ROWS = 80

bf16 = jnp.bfloat16
f32 = jnp.float32


def _dot(a, b):
    return jnp.dot(a, b, preferred_element_type=f32)


def _dot_nt(a, b):
    return lax.dot_general(a, b, (((1,), (1,)), ((), ())), preferred_element_type=f32)


def _rms(x, g):
    ms = jnp.mean(x * x, axis=-1, keepdims=True)
    return x * lax.rsqrt(ms + RMS_EPS) * g


def _split3(x):
    hi = x.astype(bf16)
    r1 = x - hi.astype(f32)
    mid = r1.astype(bf16)
    lo = (r1 - mid.astype(f32)).astype(bf16)
    return hi, mid, lo


def _tri(n):
    r = lax.broadcasted_iota(jnp.int32, (n, n), 0)
    c = lax.broadcasted_iota(jnp.int32, (n, n), 1)
    return jnp.where(c <= r, 1.0, 0.0).astype(bf16)


def _prefix_sum(t, x):
    hi, mid, lo = _split3(x)
    return _dot(t, hi) + _dot(t, mid) + _dot(t, lo)


def _log_sigmoid(z):
    return jnp.minimum(z, 0.0) - jnp.log1p(jnp.exp(-jnp.abs(z)))


def _resident(shape):
    nd = len(shape)
    return pl.BlockSpec(shape, lambda *_: (0,) * nd, pipeline_mode=pl.Buffered(1))


def _layer(shape, layer):
    nd = len(shape)
    return pl.BlockSpec((None,) + tuple(shape), lambda *_: (layer,) + (0,) * nd, pipeline_mode=pl.Buffered(1))


def _params(n_axes):
    return pltpu.CompilerParams(dimension_semantics=("arbitrary",) * n_axes,
                                vmem_limit_bytes=VMEM_LIMIT)


def _placement_constants():
    pk = np.zeros((3, LANES, N_HEADS * LANES), np.float32)
    ck_const = np.zeros((1, N_HEADS * LANES), np.float32)
    pq = np.zeros((3, LANES, LANES), np.float32)
    cq_const = np.zeros((1, LANES), np.float32)
    for j in range(3):
        cq_const[0, CK_LANE0 + j] = 1.0
        for h in range(N_HEADS):
            pk[j, h, h * LANES + CK_LANE0 + j] = -1.0
            ck_const[0, h * LANES + CQ_LANE0 + 16 * j + h] = 1.0
            pq[j, h, CQ_LANE0 + 16 * j + h] = 1.0
    expand = np.zeros((N_HEADS * 16, N_HEADS), np.float32)
    for h in range(N_HEADS):
        expand[h * 16:(h + 1) * 16, h] = 1.0
    head_sum = np.zeros((D_MODEL, LANES), np.float32)
    for h in range(N_HEADS):
        head_sum[h * HEAD_DIM:(h + 1) * HEAD_DIM, h] = 1.0
    return (jnp.asarray(pk, bf16), jnp.asarray(ck_const), jnp.asarray(pq, bf16),
            jnp.asarray(cq_const), jnp.asarray(expand, bf16), jnp.asarray(head_sum, bf16))


def _head_norm2(x, head_sum):
    xb = x.astype(bf16).astype(f32)
    return _dot((xb * xb).astype(bf16), head_sum) * NORM_SLACK


def _pack_heads(x, extras_of_head, out_ref):
    lane = lax.broadcasted_iota(jnp.int32, (x.shape[0], LANES), 1)
    for h in range(N_HEADS):
        pair = x[:, (h // 2) * LANES:(h // 2 + 1) * LANES]
        base = pair if h % 2 == 0 else pltpu.roll(pair, HEAD_DIM, axis=1)
        out_ref[h] = jnp.where(lane < HEAD_DIM, base, extras_of_head(h)).astype(bf16)


def _mixer_kernel(*refs, tm, carry_rows, seq_len):
    if carry_rows:
        x_ref, g_ref, win_ref, cw_ref, wout_ref, o_ref, u_ref, u_scr = refs
    else:
        x_ref, g_ref, win_ref, cw_ref, wout_ref, s1_ref, s2_ref, o_ref, u_ref, u_scr = refs

    @pl.when(pl.program_id(0) == 0)
    def _():
        u_scr[0:8, :] = jnp.zeros((8, D_MODEL), f32)

    x = x_ref[...]
    h = _rms(x, g_ref[...]).astype(bf16)
    bcx = _dot(h, win_ref[...])
    gb = bcx[:, 0:D_MODEL]
    u = bcx[:, D_MODEL:2 * D_MODEL] * bcx[:, 2 * D_MODEL:3 * D_MODEL]
    u_scr[8:8 + tm, :] = u
    u1 = u_scr[7:7 + tm, :]
    u2 = u_scr[6:6 + tm, :]
    if carry_rows:
        tail = u_scr[tm:tm + 8, :]
        u_scr[0:8, :] = tail
        u_ref[...] = tail
    else:
        pos = lax.broadcasted_iota(jnp.int32, (tm, 1), 0) % seq_len
        u1 = jnp.where(pos < 1, s1_ref[...], u1)
        u2 = jnp.where(pos < 2, s2_ref[...], u2)
        u_ref[...] = u
    cw = cw_ref[...]
    conv = cw[0:1, :] * u2 + cw[1:2, :] * u1 + cw[2:3, :] * u
    o_ref[...] = x + _dot((gb * conv).astype(bf16), wout_ref[...])


def _mixer(x, g, w_in, cw, w_out, layer, state_rows=None, seq_len=None):
    n = x.shape[0]
    tm = min(ROW_TILE, n)
    carry = state_rows is None
    row = pl.BlockSpec((tm, D_MODEL), lambda i: (i, 0))
    in_specs = [row, _resident((1, D_MODEL)), _layer((D_MODEL, 3 * D_MODEL), layer),
                _layer((3, D_MODEL), layer), _layer((D_MODEL, D_MODEL), layer)]
    args = [x, g, w_in, cw, w_out]
    if carry:
        u_spec = pl.BlockSpec((8, D_MODEL), lambda i: (0, 0))
        u_shape = jax.ShapeDtypeStruct((8, D_MODEL), f32)
    else:
        assert n == tm
        in_specs += [row, row]
        args += list(state_rows)
        u_spec = row
        u_shape = jax.ShapeDtypeStruct((n, D_MODEL), f32)
    return pl.pallas_call(
        functools.partial(_mixer_kernel, tm=tm, carry_rows=carry, seq_len=seq_len),
        grid=(n // tm,),
        in_specs=in_specs,
        out_specs=[row, u_spec],
        out_shape=[jax.ShapeDtypeStruct((n, D_MODEL), f32), u_shape],
        scratch_shapes=[pltpu.VMEM((tm + 8, D_MODEL), f32)],
        compiler_params=_params(1),
        name="mixer",
    )(*args)


def _mlp_kernel(*refs, has_attn, final_norm):
    refs = list(refs)
    x_ref = refs.pop(0)
    if has_attn:
        attn_ref = refs.pop(0)
        wo_ref = refs.pop(0)
    g_ref, wup_ref, wdn_ref = refs[0:3]
    gf_ref = refs[3] if final_norm else None
    o_ref = refs[-1]

    x = x_ref[...]
    if has_attn:
        x = x + _dot(attn_ref[...], wo_ref[...])
    h = _rms(x, g_ref[...]).astype(bf16)
    a = jnp.maximum(_dot(h, wup_ref[...]), 0.0)
    y = x + _dot((a * a).astype(bf16), wdn_ref[...])
    if final_norm:
        y = _rms(y, gf_ref[...])
    o_ref[...] = y


def _mlp(x, g, w_up, w_down, layer, attn=None, w_o=None, attn_layer=None, g_final=None):
    n = x.shape[0]
    tm = min(ROW_TILE, n)
    row = pl.BlockSpec((tm, D_MODEL), lambda i: (i, 0))
    in_specs, args = [row], [x]
    if attn is not None:
        in_specs += [row, _layer((D_MODEL, D_MODEL), attn_layer)]
        args += [attn, w_o]
    in_specs += [_resident((1, D_MODEL)), _layer((D_MODEL, D_FF), layer), _layer((D_FF, D_MODEL), layer)]
    args += [g, w_up, w_down]
    if g_final is not None:
        in_specs.append(_resident((1, D_MODEL)))
        args.append(g_final)
    return pl.pallas_call(
        functools.partial(_mlp_kernel, has_attn=attn is not None, final_norm=g_final is not None),
        grid=(n // tm,),
        in_specs=in_specs,
        out_specs=row,
        out_shape=jax.ShapeDtypeStruct((n, D_MODEL), f32),
        compiler_params=_params(1),
        name="mlp",
    )(*args)


def _kv_kernel(*refs, tm, pack):
    if pack:
        (x_ref, g_ref, wk_ref, wv_ref, wf_ref, bf_ref, pk_ref, ckc_ref, hs_ref,
         k_ref, v_ref, lf_ref, c_ref, kp_ref, vt_ref, n2_ref, carry) = refs
    else:
        x_ref, g_ref, wk_ref, wv_ref, wf_ref, bf_ref, k_ref, v_ref, lf_ref = refs

    hk = _rms(x_ref[...], g_ref[...]).astype(bf16)
    k = _dot(hk, wk_ref[...])
    v = _dot(hk, wv_ref[...])
    z = _dot(hk, wf_ref[...]) + bf_ref[...]
    lane = lax.broadcasted_iota(jnp.int32, (tm, LANES), 1)
    logf = jnp.where(lane < N_HEADS, _log_sigmoid(z), 0.0)
    k_ref[...] = k
    v_ref[...] = v
    lf_ref[...] = logf[:, 0:N_HEADS]
    if not pack:
        return

    @pl.when(pl.program_id(0) == 0)
    def _():
        carry[...] = jnp.zeros_like(carry)

    c = _prefix_sum(_tri(tm), logf) + carry[0:1, :]
    carry[0:1, :] = c[tm - 1:tm, :]
    c2 = c * LOG2E
    c_ref[...] = c2
    hi, mid, lo = _split3(c2)
    extras = _dot(hi, pk_ref[0]) + _dot(mid, pk_ref[1]) + _dot(lo, pk_ref[2]) + ckc_ref[...]
    _pack_heads(k, lambda h: extras[:, h * LANES:(h + 1) * LANES], kp_ref)
    n2_ref[...] = _head_norm2(k, hs_ref[...])
    pad_row = lax.broadcasted_iota(jnp.int32, (VT_ROWS - HEAD_DIM, tm), 0)
    ones_row = jnp.where(pad_row == 0, 1.0, 0.0)
    for p in range(N_HEADS // 2):
        vt = jnp.transpose(v[:, p * LANES:(p + 1) * LANES])
        for hh in range(2):
            rows = vt[hh * HEAD_DIM:(hh + 1) * HEAD_DIM, :]
            vt_ref[2 * p + hh, 0] = jnp.concatenate([rows, ones_row], axis=0).astype(bf16)


def _kv(x, g, w_k, w_v, w_f, b_f, pk=None, ck_const=None, head_sum=None):
    n = x.shape[0]
    tm = min(ROW_TILE, n)
    pack = pk is not None
    row = pl.BlockSpec((tm, D_MODEL), lambda i: (i, 0))
    heads = pl.BlockSpec((N_HEADS, tm, LANES), lambda i: (0, i, 0))
    in_specs = [row, _resident((1, D_MODEL)), _resident((D_MODEL, D_MODEL)), _resident((D_MODEL, D_MODEL)),
                _resident((D_MODEL, LANES)), _resident((1, LANES))]
    args = [x, g, w_k, w_v, w_f, b_f]
    out_specs = [row, row, pl.BlockSpec((tm, N_HEADS), lambda i: (i, 0))]
    out_shape = [jax.ShapeDtypeStruct((n, D_MODEL), f32), jax.ShapeDtypeStruct((n, D_MODEL), f32),
                 jax.ShapeDtypeStruct((n, N_HEADS), f32)]
    scratch = []
    if pack:
        lanes = pl.BlockSpec((tm, LANES), lambda i: (i, 0))
        in_specs += [_resident((3, LANES, N_HEADS * LANES)), _resident((1, N_HEADS * LANES)),
                     _resident((D_MODEL, LANES))]
        args += [pk, ck_const, head_sum]
        out_specs += [lanes, heads, pl.BlockSpec((N_HEADS, 1, VT_ROWS, tm), lambda i: (0, i, 0, 0)), lanes]
        out_shape += [jax.ShapeDtypeStruct((n, LANES), f32),
                      jax.ShapeDtypeStruct((N_HEADS, n, LANES), bf16),
                      jax.ShapeDtypeStruct((N_HEADS, n // tm, VT_ROWS, tm), bf16),
                      jax.ShapeDtypeStruct((n, LANES), f32)]
        scratch = [pltpu.VMEM((8, LANES), f32)]
    return pl.pallas_call(
        functools.partial(_kv_kernel, tm=tm, pack=pack),
        grid=(n // tm,),
        in_specs=in_specs,
        out_specs=out_specs,
        out_shape=out_shape,
        scratch_shapes=scratch,
        compiler_params=_params(1),
        name="kv",
    )(*args)


def _qproj_kernel(*refs, pack):
    if pack:
        x_ref, g_ref, wq_ref, c_ref, pq_ref, cqc_ref, hs_ref, o_ref, n2_ref = refs
    else:
        x_ref, g_ref, wq_ref, o_ref = refs
    h = _rms(x_ref[...], g_ref[...]).astype(bf16)
    q = _dot(h, wq_ref[...]) * Q_SCALE
    if not pack:
        o_ref[...] = q
        return
    hi, mid, lo = _split3(c_ref[...])
    extras = _dot(hi, pq_ref[0]) + _dot(mid, pq_ref[1]) + _dot(lo, pq_ref[2]) + cqc_ref[...]
    _pack_heads(q, lambda h: extras, o_ref)
    n2_ref[...] = _head_norm2(q, hs_ref[...])


def _qproj(x, g, w_q, layer, c2=None, pq=None, cq_const=None, head_sum=None):
    n = x.shape[0]
    tm = min(ROW_TILE, n)
    pack = c2 is not None
    row = pl.BlockSpec((tm, D_MODEL), lambda i: (i, 0))
    in_specs = [row, _resident((1, D_MODEL)), _layer((D_MODEL, D_MODEL), layer)]
    args = [x, g, w_q]
    if pack:
        lanes = pl.BlockSpec((tm, LANES), lambda i: (i, 0))
        in_specs += [lanes, _resident((3, LANES, LANES)), _resident((1, LANES)), _resident((D_MODEL, LANES))]
        args += [c2, pq, cq_const, head_sum]
        out_spec = [pl.BlockSpec((N_HEADS, tm, LANES), lambda i: (0, i, 0)), lanes]
        out_shape = [jax.ShapeDtypeStruct((N_HEADS, n, LANES), bf16), jax.ShapeDtypeStruct((n, LANES), f32)]
    else:
        out_spec = row
        out_shape = jax.ShapeDtypeStruct((n, D_MODEL), f32)
    return pl.pallas_call(
        functools.partial(_qproj_kernel, pack=pack),
        grid=(n // tm,),
        in_specs=in_specs,
        out_specs=out_spec,
        out_shape=out_shape,
        compiler_params=_params(1),
        name="qproj",
    )(*args)


def _first_needed_tile(qn2, kn2, c2, t):
    n = qn2.shape[0]
    tq = 2 * t
    nq, nk = n // tq, n // t
    qn, kn, c = jnp.sqrt(qn2[:, :N_HEADS]), jnp.sqrt(kn2[:, :N_HEADS]), c2[:, :N_HEADS]
    q_max = qn.reshape(nq, tq, N_HEADS).max(axis=1)
    k_max = kn.reshape(nk, t, N_HEADS).max(axis=1)
    cq_max = c.reshape(nq, tq, N_HEADS).max(axis=1)
    ck_min = c.reshape(nk, t, N_HEADS).min(axis=1)
    m_low = -q_max * k_max.reshape(nq, 2, N_HEADS).max(axis=1)
    upper = q_max[:, None, :] * k_max[None, :, :] + cq_max[:, None, :] - ck_min[None, :, :]
    tile = jnp.arange(nk)
    dead = (upper <= m_low[:, None, :] - SKIP_MARGIN) & (tile[None, :, None] < 2 * jnp.arange(nq)[:, None, None])
    live = jnp.logical_not(dead)
    live_upto = jnp.any(live[:, None, :, :] & (tile[None, :] <= tile[:, None])[None, :, :, None], axis=2)
    lead = jnp.sum(jnp.logical_not(live_upto).astype(jnp.int32), axis=1)
    return ((lead // 2) * 2).T.astype(jnp.int32)


def _flash_kernel(first_ref, q_ref, k_ref, vt_ref, o_ref, m_scr, acc_scr, s_even, s_odd, *, t):
    tq = 2 * t
    i = pl.program_id(1)
    first_a = first_ref[2 * pl.program_id(0), i]
    first_b = first_ref[2 * pl.program_id(0) + 1, i]
    both_until = jnp.maximum(first_a, first_b)
    alone_until = jnp.minimum(first_a, first_b)
    alone_head = jnp.where(first_a < first_b, 0, 1)
    for hh in range(2):
        m_scr[hh] = jnp.full((1, tq), NEG_BIG, f32)
        acc_scr[hh] = jnp.zeros((VT_ROWS, tq), f32)

    def logits(j, hh, c0):
        start = pl.multiple_of(j * t, t)
        q = q_ref[hh, c0:c0 + QUERY_CHUNK, :]
        return _dot_nt(k_ref[hh, pl.ds(start, t), :], q)

    def step(j, j_next, s_cur, s_next, heads=(0, 1), diag_offset=None):
        for hh in heads:
            for c0 in range(0, tq, QUERY_CHUNK):
                cols = slice(c0, c0 + QUERY_CHUNK)
                s = s_cur[hh, :, cols]
                s_next[hh, :, cols] = logits(j_next, hh, c0)
                if diag_offset is not None:
                    key = lax.broadcasted_iota(jnp.int32, (t, QUERY_CHUNK), 0) + (diag_offset - c0)
                    qry = lax.broadcasted_iota(jnp.int32, (t, QUERY_CHUNK), 1)
                    s = jnp.where(key <= qry, s, NEG_BIG)
                m_prev = m_scr[hh, :, cols]
                m_new = jnp.maximum(m_prev, jnp.max(s, axis=0, keepdims=True))
                p = jnp.exp2(s - m_new)
                pv = _dot(vt_ref[hh, j], p.astype(bf16))
                acc_scr[hh, :, cols] = jnp.exp2(m_prev - m_new) * acc_scr[hh, :, cols] + pv
                m_scr[hh, :, cols] = m_new

    for hh in range(2):
        for c0 in range(0, tq, QUERY_CHUNK):
            s_even[hh, :, c0:c0 + QUERY_CHUNK] = logits(2 * i, hh, c0)
    step(2 * i, 2 * i + 1, s_even, s_odd, diag_offset=0)
    step(2 * i + 1, jnp.maximum(2 * i - 1, 0), s_odd, s_even, diag_offset=t)

    def pair(n, carry):
        j = 2 * i - 1 - 2 * n
        step(j, j - 1, s_even, s_odd)
        step(j - 1, jnp.maximum(j - 2, 0), s_odd, s_even)
        return carry

    lax.fori_loop(0, (2 * i - both_until) // 2, pair, 0)

    def pair_alone(n, carry):
        j = both_until - 1 - 2 * n
        step(j, j - 1, s_even, s_odd, heads=(alone_head,))
        step(j - 1, jnp.maximum(j - 2, 0), s_odd, s_even, heads=(alone_head,))
        return carry

    lax.fori_loop(0, (both_until - alone_until) // 2, pair_alone, 0)

    outs = []
    for hh in range(2):
        acc = acc_scr[hh]
        outs.append(acc[0:HEAD_DIM, :] / acc[HEAD_DIM:HEAD_DIM + 1, :])
    o_ref[...] = jnp.transpose(jnp.concatenate(outs, axis=0)).astype(bf16)


def _flash(qp, kp, vt, first):
    n = qp.shape[1]
    t = ATT_TILE
    tq = 2 * t
    assert vt.shape == (N_HEADS, n // t, VT_ROWS, t) and n % tq == 0
    assert first.shape == (N_HEADS, n // tq)
    grid_spec = pltpu.PrefetchScalarGridSpec(
        num_scalar_prefetch=1,
        grid=(N_HEADS // 2, n // tq),
        in_specs=[pl.BlockSpec((2, tq, LANES), lambda p, i, first_ref: (p, i, 0)),
                  pl.BlockSpec((2, n, LANES), lambda p, i, first_ref: (p, 0, 0)),
                  pl.BlockSpec((2, n // t, VT_ROWS, t), lambda p, i, first_ref: (p, 0, 0, 0))],
        out_specs=pl.BlockSpec((tq, LANES), lambda p, i, first_ref: (i, p)),
        scratch_shapes=[pltpu.VMEM((2, 1, tq), f32), pltpu.VMEM((2, VT_ROWS, tq), f32),
                        pltpu.VMEM((2, t, tq), f32), pltpu.VMEM((2, t, tq), f32)])
    return pl.pallas_call(
        functools.partial(_flash_kernel, t=t),
        grid_spec=grid_spec,
        out_shape=jax.ShapeDtypeStruct((n, D_MODEL), bf16),
        compiler_params=_params(2),
        name="flash",
    )(first, qp, kp, vt)


def _cum_kernel(past_ref, new_ref, o_ref, *, past_len, new_len, chunk):
    t = _tri(chunk)
    carry = jnp.zeros((1, N_HEADS), f32)
    for s in range(0, past_len, chunk):
        c = _prefix_sum(t, past_ref[0, s:s + chunk, :]) + carry
        o_ref[0, s:s + chunk, :] = c * LOG2E
        carry = c[chunk - 1:chunk, :]
    c = _prefix_sum(_tri(new_len), new_ref[0]) + carry
    o_ref[0, past_len:past_len + new_len, :] = c * LOG2E


def _cum(past_logf, new_logf):
    b, past_len, _ = past_logf.shape
    new_len = new_logf.shape[1]
    total = past_len + new_len
    return pl.pallas_call(
        functools.partial(_cum_kernel, past_len=past_len, new_len=new_len, chunk=512),
        grid=(b,),
        in_specs=[pl.BlockSpec((1, past_len, N_HEADS), lambda i: (i, 0, 0)),
                  pl.BlockSpec((1, new_len, N_HEADS), lambda i: (i, 0, 0))],
        out_specs=pl.BlockSpec((1, total, N_HEADS), lambda i: (i, 0, 0)),
        out_shape=jax.ShapeDtypeStruct((b, total, N_HEADS), f32),
        compiler_params=_params(1),
        name="cum",
    )(past_logf, new_logf)


def _expand_heads(e, x):
    hi, mid, lo = _split3(x)
    return _dot_nt(e, hi) + _dot_nt(e, mid) + _dot_nt(e, lo)


def _decode_attn_kernel(q_ref, kc_ref, vc_ref, kn_ref, vn_ref, c_ref, e_ref, o_ref, *, past_len, new_len):
    rows = N_HEADS * new_len
    q = q_ref[0]
    row_head = lax.broadcasted_iota(jnp.int32, (rows, D_MODEL), 0) // new_len
    col_head = lax.broadcasted_iota(jnp.int32, (rows, D_MODEL), 1) // HEAD_DIM
    qbd = jnp.where(row_head == col_head, jnp.tile(q, (N_HEADS, 1)), 0.0).astype(bf16)
    e = e_ref[...]
    c2 = c_ref[0]
    ck_new = _expand_heads(e, c2[past_len:past_len + new_len, :])
    t_of_row = lax.broadcasted_iota(jnp.int32, (rows, new_len), 0) % new_len
    s_idx = lax.broadcasted_iota(jnp.int32, (rows, new_len), 1)
    cq = jnp.sum(jnp.where(s_idx == t_of_row, ck_new, 0.0), axis=1, keepdims=True)

    lg = _dot_nt(qbd, kc_ref[0]) + cq - _expand_heads(e, c2[0:past_len, :])
    lg_new = jnp.where(s_idx <= t_of_row, _dot_nt(qbd, kn_ref[0].astype(bf16)) + cq - ck_new, NEG_BIG)
    m = jnp.maximum(jnp.max(lg, axis=1, keepdims=True), jnp.max(lg_new, axis=1, keepdims=True))
    p = jnp.exp2(lg - m)
    p_new = jnp.exp2(lg_new - m)
    denom = jnp.sum(p, axis=1, keepdims=True) + jnp.sum(p_new, axis=1, keepdims=True)
    o = (_dot(p.astype(bf16), vc_ref[0])
         + _dot(p_new.astype(bf16), vn_ref[0].astype(bf16))) / denom
    lane = lax.broadcasted_iota(jnp.int32, (new_len, LANES), 1)
    for pp in range(N_HEADS // 2):
        even = o[(2 * pp) * new_len:(2 * pp + 1) * new_len, pp * LANES:(pp + 1) * LANES]
        odd = o[(2 * pp + 1) * new_len:(2 * pp + 2) * new_len, pp * LANES:(pp + 1) * LANES]
        o_ref[0, :, pp * LANES:(pp + 1) * LANES] = jnp.where(lane < HEAD_DIM, even, odd).astype(bf16)


def _decode_attn(q, cache_k, cache_v, k_new, v_new, c2, expand):
    b, past_len, _ = cache_k.shape
    new_len = q.shape[1]
    cache = pl.BlockSpec((1, past_len, D_MODEL), lambda i: (i, 0, 0))
    new = pl.BlockSpec((1, new_len, D_MODEL), lambda i: (i, 0, 0))
    return pl.pallas_call(
        functools.partial(_decode_attn_kernel, past_len=past_len, new_len=new_len),
        grid=(b,),
        in_specs=[new, cache, cache, new, new,
                  pl.BlockSpec((1, past_len + new_len, N_HEADS), lambda i: (i, 0, 0)),
                  _resident((N_HEADS * new_len, N_HEADS))],
        out_specs=new,
        out_shape=jax.ShapeDtypeStruct((b, new_len, D_MODEL), bf16),
        compiler_params=_params(1),
        name="decode_attn",
    )(q, cache_k, cache_v, k_new, v_new, c2, expand)


def kernel(x_prompt, x_sample, cache_k, cache_v, cache_logf, state_conv, norm_mix, norm_mlp, conv_in, conv_w,
           conv_out, norm_kv, w_k, w_v, w_f, b_f, w_q, w_o, mlp_up, mlp_down, norm_final):
    assert x_prompt.shape[0] == 1 and x_prompt.shape[2] == D_MODEL
    seq = x_prompt.shape[1]
    dec_b, dec_s, _ = x_sample.shape
    past_len = cache_k.shape[1]
    assert seq % ROW_TILE == 0 and ROW_TILE == ATT_TILE and (dec_b * dec_s) % 8 == 0

    pk, ck_const, pq, cq_const, expand, head_sum = _placement_constants()
    row = lambda a: a.reshape(1, -1)
    conv_in_b, conv_out_b = conv_in.astype(bf16), conv_out.astype(bf16)
    w_k_b, w_v_b, w_q_b, w_o_b = w_k.astype(bf16), w_v.astype(bf16), w_q.astype(bf16), w_o.astype(bf16)
    up_b, down_b = mlp_up.astype(bf16), mlp_down.astype(bf16)
    w_f_b = jnp.pad(w_f, ((0, 0), (0, LANES - N_HEADS))).astype(bf16)
    b_f_p = jnp.pad(b_f, (0, LANES - N_HEADS)).reshape(1, LANES)

    def mlp(x, l, **kw):
        return _mlp(x, row(norm_mlp[l]), up_b, down_b, l, **kw)

    x = x_prompt.reshape(seq, D_MODEL)
    tails = []
    for l in range(N_A):
        x, tail = _mixer(x, row(norm_mix[l]), conv_in_b, conv_w, conv_out_b, l)
        tails.append(tail[6:8])
        x = mlp(x, l)
    k_p, v_p, logf_p, c2_p, kp, vt, kn2 = _kv(x, row(norm_kv), w_k_b, w_v_b, w_f_b, b_f_p, pk, ck_const, head_sum)
    for l in range(N_A, DEPTH):
        j = l - N_A
        qp, qn2 = _qproj(x, row(norm_mix[l]), w_q_b, j, c2_p, pq, cq_const, head_sum)
        attn = _flash(qp, kp, vt, _first_needed_tile(qn2, kn2, c2_p, ATT_TILE))
        x = mlp(x, l, attn=attn, w_o=w_o_b, attn_layer=j,
                g_final=row(norm_final) if l == DEPTH - 1 else None)
    y_prompt = x.reshape(1, seq, D_MODEL)
    conv_prompt = jnp.stack(tails).reshape(N_A, 1, 2, D_MODEL)

    n_s = dec_b * dec_s
    x = x_sample.reshape(n_s, D_MODEL)
    new_states = []
    for l in range(N_A):
        st = state_conv[l]
        pad = jnp.zeros((dec_b, dec_s - 2, D_MODEL), f32)
        s2 = jnp.concatenate([st, pad], axis=1).reshape(n_s, D_MODEL)
        s1 = jnp.concatenate([st[:, 1:2], pad, pad[:, 0:1]], axis=1).reshape(n_s, D_MODEL)
        x, u = _mixer(x, row(norm_mix[l]), conv_in_b, conv_w, conv_out_b, l,
                      state_rows=(s1, s2), seq_len=dec_s)
        new_states.append(u.reshape(dec_b, dec_s, D_MODEL)[:, dec_s - 2:])
        x = mlp(x, l)
    k_s, v_s, logf_s = _kv(x, row(norm_kv), w_k_b, w_v_b, w_f_b, b_f_p)
    c2_s = _cum(cache_logf, logf_s.reshape(dec_b, dec_s, N_HEADS))
    k_s3, v_s3 = k_s.reshape(dec_b, dec_s, D_MODEL), v_s.reshape(dec_b, dec_s, D_MODEL)
    cache_k3 = cache_k.astype(bf16).reshape(dec_b, past_len, D_MODEL)
    cache_v3 = cache_v.astype(bf16).reshape(dec_b, past_len, D_MODEL)
    for l in range(N_A, DEPTH):
        j = l - N_A
        q = _qproj(x, row(norm_mix[l]), w_q_b, j)
        attn = _decode_attn(q.reshape(dec_b, dec_s, D_MODEL), cache_k3, cache_v3, k_s3, v_s3, c2_s, expand)
        x = mlp(x, l, attn=attn.reshape(n_s, D_MODEL), w_o=w_o_b, attn_layer=j,
                g_final=row(norm_final) if l == DEPTH - 1 else None)
    y_sample = x.reshape(dec_b, dec_s, D_MODEL)

    return (y_prompt, y_sample,
            k_p.reshape(1, seq, N_HEADS, HEAD_DIM), v_p.reshape(1, seq, N_HEADS, HEAD_DIM),
            logf_p.reshape(1, seq, N_HEADS), conv_prompt,
            k_s.reshape(dec_b, dec_s, N_HEADS, HEAD_DIM), v_s.reshape(dec_b, dec_s, N_HEADS, HEAD_DIM),
            logf_s.reshape(dec_b, dec_s, N_HEADS), jnp.stack(new_states))
```
